```python
import math
import jax, jax.numpy as jnp
from jax import lax
import numpy as np

D_MODEL = 2048
BATCH = 16
SEQ = 2048
DEPTH = 4
DEC_BATCH = 1
DEC_SEQ = 8192
PAST_LEN = 128

N_MIXERS = 2
HQ = 16
HKV = 4
HD = 128
GROUP = HQ // HKV
WINDOW = 128
BLOCK = 128
ROT_DIM = HD // 4
ROPE_THETA = 500000.0
QKV_DIM = (HQ + 2 * HKV) * HD
HK = 16
HV = 32
DK = 128
DV = 128
QK_DIM = HK * DK
V_DIM = HV * DV
CONV_DIM = 2 * QK_DIM + V_DIM
CONV_K = 5
CONV_PAD = CONV_K // 2
CHUNK = 64
DN_IN_DIM = CONV_DIM + V_DIM + 2 * 2 * HV
N_EXPERTS = 16
D_EXPERT = 1024
CAP_FACTOR = 2
DEEPNORM_ALPHA = (2.0 * DEPTH) ** 0.25
DEEPNORM_BETA = (8.0 * DEPTH) ** -0.25
LN_EPS = 1e-5
RMS_EPS = 1e-6

kernel_name = "hybrid_bidir_swa_gdn_ec_encoder"


def layer_norm(x, g, b):
    xf = x.astype(jnp.float32)
    mu = jnp.mean(xf, axis=-1, keepdims=True)
    var = jnp.mean(jnp.square(xf - mu), axis=-1, keepdims=True)
    y = (xf - mu) * lax.rsqrt(var + LN_EPS) * g.astype(jnp.float32) + b.astype(jnp.float32)
    return y.astype(x.dtype)


def rope_tables(T, dtype):
    inv = ROPE_THETA ** (-jnp.arange(0, ROT_DIM, 2, dtype=jnp.float32) / ROT_DIM)
    ang = jnp.arange(T, dtype=jnp.float32)[:, None] * inv[None, :]
    return jnp.cos(ang).astype(dtype), jnp.sin(ang).astype(dtype)


def apply_partial_rope(t, cos, sin):
    half = ROT_DIM // 2
    t1 = t[..., :half]
    t2 = t[..., half:ROT_DIM]
    c = cos[None, :, None, :]
    s = sin[None, :, None, :]
    return jnp.concatenate([t1 * c - t2 * s, t2 * c + t1 * s, t[..., ROT_DIM:]], axis=-1)


def windowed_gqa_sink(x, w_qkv, w_o, sink, cos, sin):
    B, T, _ = x.shape
    nb = T // BLOCK
    qkv = x @ w_qkv
    q = qkv[..., :HQ * HD].reshape(B, T, HQ, HD)
    k = qkv[..., HQ * HD:(HQ + HKV) * HD].reshape(B, T, HKV, HD)
    v = qkv[..., (HQ + HKV) * HD:].reshape(B, T, HKV, HD)
    q = apply_partial_rope(q, cos, sin)
    k = apply_partial_rope(k, cos, sin)
    q = q.reshape(B, nb, BLOCK, HKV, GROUP, HD)

    def band(t):
        tp = jnp.pad(t, ((0, 0), (BLOCK, BLOCK), (0, 0), (0, 0))).reshape(B, nb + 2, BLOCK, HKV, HD)
        return jnp.concatenate([tp[:, :-2], tp[:, 1:-1], tp[:, 2:]], axis=2)

    kb = band(k)
    vb = band(v)
    s = jnp.einsum('bnqhgd,bnkhd->bnhgqk', q, kb).astype(jnp.float32) * (HD ** -0.5)
    qi = jnp.arange(BLOCK)[None, :, None]
    kj = jnp.arange(3 * BLOCK)[None, None, :]
    blk = jnp.arange(nb)[:, None, None]
    qpos = blk * BLOCK + qi
    kpos = (blk - 1) * BLOCK + kj
    valid = (jnp.abs(qpos - kpos) <= WINDOW) & (kpos >= 0) & (kpos < T)
    s = jnp.where(valid[None, :, None, None], s, -jnp.inf)
    sk = sink.astype(jnp.float32).reshape(HKV, GROUP)[None, None, :, :, None, None]
    m = jnp.maximum(jnp.max(s, axis=-1, keepdims=True), sk)
    p = jnp.exp(s - m)
    denom = jnp.sum(p, axis=-1, keepdims=True) + jnp.exp(sk - m)
    p = (p / denom).astype(x.dtype)
    o = jnp.einsum('bnhgqk,bnkhd->bnqhgd', p, vb).reshape(B, T, HQ * HD)
    return o @ w_o


def l2norm(t):
    return t * lax.rsqrt(jnp.sum(t * t, axis=-1, keepdims=True) + RMS_EPS)


def chunk_gated_delta(q, k, v, g, beta):
    B, T, H, dk = q.shape
    dv = v.shape[-1]
    N = T // CHUNK

    def blk(t):
        return t.reshape(B, N, CHUNK, H, t.shape[-1]).transpose(1, 0, 3, 2, 4)

    q, k, v = blk(q), blk(k), blk(v)
    g = jnp.cumsum(blk(g[..., None])[..., 0], axis=-1)
    beta = blk(beta[..., None])[..., 0]
    idx = jnp.arange(CHUNK)
    causal = idx[:, None] >= idx[None, :]
    strict = idx[:, None] > idx[None, :]
    decay = jnp.exp(jnp.where(causal, g[..., :, None] - g[..., None, :], -jnp.inf))
    kbeta = k * beta[..., None]
    L = jnp.where(strict, jnp.einsum('nbhcd,nbhsd->nbhcs', kbeta, k) * decay, 0.0)
    A = L + jnp.eye(CHUNK, dtype=L.dtype)
    rhs = jnp.concatenate([v * beta[..., None], kbeta * jnp.exp(g)[..., None]], axis=-1)
    sol = lax.linalg.triangular_solve(A, rhs, left_side=True, lower=True, unit_diagonal=True)
    u = sol[..., :dv]
    w = sol[..., dv:]
    intra = jnp.einsum('nbhcd,nbhsd->nbhcs', q, k) * decay
    q_dec = q * jnp.exp(g)[..., None]
    g_last = g[..., -1:]
    k_dec = k * jnp.exp(g_last - g)[..., None]

    def step(S, xs):
        u_n, w_n, qd_n, kd_n, in_n, gl_n = xs
        v_new = u_n - jnp.einsum('bhcd,bhde->bhce', w_n, S)
        o = jnp.einsum('bhcd,bhde->bhce', qd_n, S) + jnp.einsum('bhcs,bhse->bhce', in_n, v_new)
        S = S * jnp.exp(gl_n)[..., None] + jnp.einsum('bhcd,bhce->bhde', kd_n, v_new)
        return S, o

    S0 = jnp.zeros((B, H, dk, dv), jnp.float32)
    _, o = lax.scan(step, S0, (u, w, q_dec, k_dec, intra, g_last))
    return o.transpose(1, 0, 3, 2, 4).reshape(B, T, H, dv)


def gated_deltanet_bidir(x, w_in, conv_w, a_log, dt_bias, norm_w, w_out):
    B, T, _ = x.shape
    proj = x @ w_in
    qkv = proj[..., :CONV_DIM]
    z = proj[..., CONV_DIM:CONV_DIM + V_DIM]
    ab = proj[..., CONV_DIM + V_DIM:].astype(jnp.float32).reshape(B, T, 2, 2, HV)
    qkv = lax.conv_general_dilated(qkv, conv_w[:, None, :].astype(qkv.dtype), (1,), [(CONV_PAD, CONV_PAD)],
                                   dimension_numbers=('NWC', 'WIO', 'NWC'), feature_group_count=CONV_DIM)
    qkv = jax.nn.silu(qkv).astype(jnp.float32)
    q = l2norm(qkv[..., :QK_DIM].reshape(B, T, HK, DK)) * (DK ** -0.5)
    k = l2norm(qkv[..., QK_DIM:2 * QK_DIM].reshape(B, T, HK, DK))
    v = qkv[..., 2 * QK_DIM:].reshape(B, T, HV, DV)
    q = jnp.repeat(q, HV // HK, axis=2)
    k = jnp.repeat(k, HV // HK, axis=2)
    g = -jnp.exp(a_log.astype(jnp.float32))[None, None] * jax.nn.softplus(ab[:, :, :, 0] + dt_bias.astype(jnp.float32)[None, None])
    beta = jax.nn.sigmoid(ab[:, :, :, 1])
    o_fwd = chunk_gated_delta(q, k, v, g[:, :, 0], beta[:, :, 0])
    flip = lambda t: jnp.flip(t, axis=1)
    o_bwd = flip(chunk_gated_delta(flip(q), flip(k), flip(v), flip(g[:, :, 1]), flip(beta[:, :, 1])))
    o = o_fwd + o_bwd
    o = o * lax.rsqrt(jnp.mean(o * o, axis=-1, keepdims=True) + RMS_EPS) * norm_w.astype(jnp.float32)
    o = o * jax.nn.silu(z.reshape(B, T, HV, DV).astype(jnp.float32))
    return o.reshape(B, T, V_DIM).astype(x.dtype) @ w_out


def expert_choice_ffn(x2d, w_router, w_gate, w_up, w_down):
    n, D = x2d.shape
    cap = CAP_FACTOR * n // N_EXPERTS
    aff = jax.nn.softmax((x2d @ w_router).astype(jnp.float32), axis=-1)
    gates, idx = lax.top_k(aff.T, cap)
    xe = x2d[idx]
    h = jax.nn.silu(jnp.einsum('ecd,edf->ecf', xe, w_gate)) * jnp.einsum('ecd,edf->ecf', xe, w_up)
    ye = jnp.einsum('ecf,efd->ecd', h, w_down) * gates[..., None].astype(x2d.dtype)
    return jnp.zeros_like(x2d).at[idx.reshape(-1)].add(ye.reshape(-1, D))


def encoder_trunk(x, attn_w_qkv, attn_w_o, attn_sink, dn_w_in, dn_conv_w, dn_a_log, dn_dt_bias, dn_norm_w,
                  dn_w_out, moe_w_router, moe_w_gate, moe_w_up, moe_w_down, ln_mix_g, ln_mix_b, ln_ffn_g, ln_ffn_b):
    B, T, D = x.shape
    cos, sin = rope_tables(T, x.dtype)
    for i in range(DEPTH):
        j = i // N_MIXERS
        if i % N_MIXERS == 0:
            h = windowed_gqa_sink(x, attn_w_qkv[j], attn_w_o[j], attn_sink[j], cos, sin)
        else:
            h = gated_deltanet_bidir(x, dn_w_in[j], dn_conv_w[j], dn_a_log[j], dn_dt_bias[j], dn_norm_w[j], dn_w_out[j])
        x = layer_norm(DEEPNORM_ALPHA * x + h, ln_mix_g[i], ln_mix_b[i])
        h = expert_choice_ffn(x.reshape(B * T, D), moe_w_router[i], moe_w_gate[i], moe_w_up[i], moe_w_down[i]).reshape(B, T, D)
        x = layer_norm(DEEPNORM_ALPHA * x + h, ln_ffn_g[i], ln_ffn_b[i])
    return x


def setup_inputs(seed: int = 0) -> dict:
    key = jax.random.key(seed)
    ks = jax.random.split(key, 24)
    f32 = jnp.float32
    n_attn = (DEPTH + N_MIXERS - 1) // N_MIXERS
    n_dn = DEPTH // N_MIXERS

    def nrm(k, shape, scale):
        return jax.random.normal(k, shape, f32) * scale

    dt = jnp.exp(jax.random.uniform(ks[8], (n_dn, 2, HV), f32, math.log(1e-3), math.log(1e-1)))
    return {
        "x_prompt": nrm(ks[0], (BATCH, SEQ, D_MODEL), 1.0),
        "x_sample": nrm(ks[1], (DEC_BATCH, DEC_SEQ, D_MODEL), 1.0),
        "attn_w_qkv": nrm(ks[2], (n_attn, D_MODEL, QKV_DIM), D_MODEL ** -0.5),
        "attn_w_o": nrm(ks[3], (n_attn, HQ * HD, D_MODEL), (HQ * HD) ** -0.5 * DEEPNORM_BETA),
        "attn_sink": nrm(ks[4], (n_attn, HQ), 0.5),
        "dn_w_in": nrm(ks[5], (n_dn, D_MODEL, DN_IN_DIM), D_MODEL ** -0.5),
        "dn_conv_w": nrm(ks[6], (n_dn, CONV_K, CONV_DIM), CONV_K ** -0.5),
        "dn_a_log": jnp.log(jax.random.uniform(ks[7], (n_dn, 2, HV), f32, 1.0, 16.0)),
        "dn_dt_bias": dt + jnp.log(-jnp.expm1(-dt)),
        "dn_norm_w": 1.0 + nrm(ks[9], (n_dn, DV), 0.02),
        "dn_w_out": nrm(ks[10], (n_dn, V_DIM, D_MODEL), V_DIM ** -0.5 * DEEPNORM_BETA),
        "moe_w_router": nrm(ks[11], (DEPTH, D_MODEL, N_EXPERTS), D_MODEL ** -0.5),
        "moe_w_gate": nrm(ks[12], (DEPTH, N_EXPERTS, D_MODEL, D_EXPERT), D_MODEL ** -0.5),
        "moe_w_up": nrm(ks[13], (DEPTH, N_EXPERTS, D_MODEL, D_EXPERT), D_MODEL ** -0.5),
        "moe_w_down": nrm(ks[14], (DEPTH, N_EXPERTS, D_EXPERT, D_MODEL), D_EXPERT ** -0.5 * DEEPNORM_BETA),
        "ln_mix_g": 1.0 + nrm(ks[15], (DEPTH, D_MODEL), 0.02),
        "ln_mix_b": nrm(ks[16], (DEPTH, D_MODEL), 0.02),
        "ln_ffn_g": 1.0 + nrm(ks[17], (DEPTH, D_MODEL), 0.02),
        "ln_ffn_b": nrm(ks[18], (DEPTH, D_MODEL), 0.02),
    }


def reference(x_prompt, x_sample, attn_w_qkv, attn_w_o, attn_sink, dn_w_in, dn_conv_w, dn_a_log, dn_dt_bias,
              dn_norm_w, dn_w_out, moe_w_router, moe_w_gate, moe_w_up, moe_w_down, ln_mix_g, ln_mix_b,
              ln_ffn_g, ln_ffn_b):
    y_prompt = encoder_trunk(x_prompt, attn_w_qkv, attn_w_o, attn_sink, dn_w_in, dn_conv_w, dn_a_log, dn_dt_bias,
                             dn_norm_w, dn_w_out, moe_w_router, moe_w_gate, moe_w_up, moe_w_down,
                             ln_mix_g, ln_mix_b, ln_ffn_g, ln_ffn_b)
    y_sample = encoder_trunk(x_sample, attn_w_qkv, attn_w_o, attn_sink, dn_w_in, dn_conv_w, dn_a_log, dn_dt_bias,
                             dn_norm_w, dn_w_out, moe_w_router, moe_w_gate, moe_w_up, moe_w_down,
                             ln_mix_g, ln_mix_b, ln_ffn_g, ln_ffn_b)
    return (y_prompt, y_sample)
```

```python
import functools
import math

import jax
import jax.numpy as jnp
from jax import lax
from jax.experimental import pallas as pl
from jax.experimental.pallas import tpu as pltpu

f32, i32 = jnp.float32, jnp.int32
MXU = jnp.bfloat16

D_MODEL = 2048
DEPTH = 4
HQ, HKV, HD = 16, 4, 128
GROUP = HQ // HKV
BLOCK = 128
ROT_DIM = HD // 4
ROPE_THETA = 500000.0
HK, HV, DK, DV = 16, 32, 128, 128
QK_DIM, V_DIM = HK * DK, HV * DV
CONV_DIM = 2 * QK_DIM + V_DIM
CONV_K = 5
CHUNK = 64
N_EXPERTS = 16
D_EXPERT = 1024
CAP_FACTOR = 2
DEEPNORM_ALPHA = (2.0 * DEPTH) ** 0.25
LN_EPS = 1e-5
RMS_EPS = 1e-6

LANES = 128
V7X_VMEM_LIMIT = 56 * 1024 * 1024


def _params(*sem):
    return pltpu.CompilerParams(dimension_semantics=sem, vmem_limit_bytes=V7X_VMEM_LIMIT)


def _mm_kernel(a_ref, w_ref, o_ref):
    o_ref[...] = jnp.dot(a_ref[...].astype(MXU), w_ref[...], preferred_element_type=f32).astype(o_ref.dtype)


def matmul(a, w, out_dtype=f32, tm=512, tn=1024):
    M, K = a.shape
    N = w.shape[1]
    tm, tn = min(tm, M), min(tn, N)
    assert M % tm == 0 and N % tn == 0
    return pl.pallas_call(
        _mm_kernel,
        grid=(N // tn, M // tm),
        in_specs=[pl.BlockSpec((tm, K), lambda j, i: (i, 0)), pl.BlockSpec((K, tn), lambda j, i: (0, j))],
        out_specs=pl.BlockSpec((tm, tn), lambda j, i: (i, j)),
        out_shape=jax.ShapeDtypeStruct((M, N), out_dtype),
        compiler_params=_params("parallel", "parallel"),
        name="matmul",
    )(a, w)


def _layer_norm_rows(y, g, b):
    mu = jnp.mean(y, axis=-1, keepdims=True)
    yc = y - mu
    var = jnp.mean(yc * yc, axis=-1, keepdims=True)
    return yc * lax.rsqrt(var + LN_EPS) * g + b


def _mm_ln_kernel(a_ref, w_ref, x_ref, g_ref, b_ref, o_ref):
    h = jnp.dot(a_ref[...].astype(MXU), w_ref[...], preferred_element_type=f32)
    o_ref[...] = _layer_norm_rows(DEEPNORM_ALPHA * x_ref[...] + h, g_ref[...], b_ref[...])


def matmul_res_ln(a, w, x, g, b, tm=256):
    M, K = a.shape
    D = w.shape[1]
    tm = min(tm, M)
    assert M % tm == 0
    return pl.pallas_call(
        _mm_ln_kernel,
        grid=(M // tm,),
        in_specs=[
            pl.BlockSpec((tm, K), lambda i: (i, 0)),
            pl.BlockSpec((K, D), lambda i: (0, 0), pipeline_mode=pl.Buffered(1)),
            pl.BlockSpec((tm, D), lambda i: (i, 0)),
            pl.BlockSpec((1, D), lambda i: (0, 0)),
            pl.BlockSpec((1, D), lambda i: (0, 0)),
        ],
        out_specs=pl.BlockSpec((tm, D), lambda i: (i, 0)),
        out_shape=jax.ShapeDtypeStruct((M, D), f32),
        compiler_params=_params("parallel"),
        name="matmul_res_ln",
    )(a, w, x, g.reshape(1, D), b.reshape(1, D))


def rope_table(T):
    half = ROT_DIM // 2
    inv = ROPE_THETA ** (-jnp.arange(0, ROT_DIM, 2, dtype=f32) / ROT_DIM)
    ang = jnp.arange(T, dtype=f32)[:, None] * inv[None, :]
    c, s = jnp.cos(ang), jnp.sin(ang)
    z = jnp.zeros((T, HD - ROT_DIM), f32)
    zh = jnp.zeros((T, half), f32)
    return jnp.concatenate([c, c, z + 1.0, -s, zh, z, zh, s, z], axis=1)


def _rope(t, tab):
    return t * tab[:, :HD] + pltpu.roll(t, HD - ROT_DIM // 2, 1) * tab[:, HD:2 * HD] + pltpu.roll(t, ROT_DIM // 2, 1) * tab[:, 2 * HD:]


def _attn_kernel(sink_ref, q_ref, kp_ref, kc_ref, kn_ref, vp_ref, vc_ref, vn_ref, tp_ref, tc_ref, tn_ref, o_ref):
    n = pl.program_id(1)
    nb = pl.num_programs(1)
    ii = lax.broadcasted_iota(i32, (GROUP * BLOCK, 3 * BLOCK), 0) % BLOCK
    jj = lax.broadcasted_iota(i32, (GROUP * BLOCK, 3 * BLOCK), 1)
    lo = jnp.where(n == 0, BLOCK, 0)
    hi = jnp.where(n == nb - 1, 2 * BLOCK, 3 * BLOCK)
    valid = (jj >= ii) & (jj <= ii + 2 * BLOCK) & (jj >= lo) & (jj < hi)
    tq = tc_ref[...]
    tabs = (tp_ref[...], tq, tn_ref[...])
    for h in range(HKV):
        hs = slice(h * HD, (h + 1) * HD)
        kb = jnp.concatenate([_rope(r[0, :, hs], t) for r, t in zip((kp_ref, kc_ref, kn_ref), tabs)], axis=0).astype(MXU)
        vb = jnp.concatenate([r[0, :, hs] for r in (vp_ref, vc_ref, vn_ref)], axis=0).astype(MXU)
        qh = jnp.concatenate([_rope(q_ref[0, :, (h * GROUP + g) * HD:(h * GROUP + g + 1) * HD], tq) for g in range(GROUP)],
                             axis=0).astype(MXU)
        s = lax.dot_general(qh, kb, (((1,), (1,)), ((), ())), preferred_element_type=f32) * (HD ** -0.5)
        s = jnp.where(valid, s, -jnp.inf)
        sk = jnp.concatenate([jnp.full((BLOCK, 1), sink_ref[0, h * GROUP + g], f32) for g in range(GROUP)], axis=0)
        m = jnp.maximum(jnp.max(s, axis=-1, keepdims=True), sk)
        p = jnp.exp(s - m)
        denom = jnp.sum(p, axis=-1, keepdims=True) + jnp.exp(sk - m)
        p = (p / denom).astype(MXU)
        o = jnp.dot(p, vb, preferred_element_type=f32)
        for g in range(GROUP):
            o_ref[0, :, (h * GROUP + g) * HD:(h * GROUP + g + 1) * HD] = o[g * BLOCK:(g + 1) * BLOCK].astype(o_ref.dtype)


def attention(qkv, sink):
    B, T, _ = qkv.shape
    out_dtype = MXU
    nb = T // BLOCK
    tab = rope_table(T)
    kcol, vcol = HQ // HKV, HQ // HKV + 1
    kvw = HKV * HD
    prev = lambda n: jnp.maximum(n - 1, 0)
    nxt = lambda n: jnp.minimum(n + 1, nb - 1)
    return pl.pallas_call(
        _attn_kernel,
        grid=(B, nb),
        in_specs=[
            pl.BlockSpec(memory_space=pltpu.SMEM),
            pl.BlockSpec((1, BLOCK, HQ * HD), lambda b, n: (b, n, 0)),
            pl.BlockSpec((1, BLOCK, kvw), lambda b, n: (b, prev(n), kcol)),
            pl.BlockSpec((1, BLOCK, kvw), lambda b, n: (b, n, kcol)),
            pl.BlockSpec((1, BLOCK, kvw), lambda b, n: (b, nxt(n), kcol)),
            pl.BlockSpec((1, BLOCK, kvw), lambda b, n: (b, prev(n), vcol)),
            pl.BlockSpec((1, BLOCK, kvw), lambda b, n: (b, n, vcol)),
            pl.BlockSpec((1, BLOCK, kvw), lambda b, n: (b, nxt(n), vcol)),
            pl.BlockSpec((BLOCK, 3 * HD), lambda b, n: (prev(n), 0)),
            pl.BlockSpec((BLOCK, 3 * HD), lambda b, n: (n, 0)),
            pl.BlockSpec((BLOCK, 3 * HD), lambda b, n: (nxt(n), 0)),
        ],
        out_specs=pl.BlockSpec((1, BLOCK, HQ * HD), lambda b, n: (b, n, 0)),
        out_shape=jax.ShapeDtypeStruct((B, T, HQ * HD), out_dtype),
        compiler_params=_params("parallel", "parallel"),
        name="attention",
    )(sink.reshape(1, HQ).astype(f32), qkv, qkv, qkv, qkv, qkv, qkv, qkv, tab, tab, tab)


def attn_layer(x, w_qkv, w_o, sink, ln_g, ln_b):
    B, T, D = x.shape
    x2 = x.reshape(B * T, D)
    qkv = matmul(x2, w_qkv).reshape(B, T, -1)
    o = attention(qkv, sink).reshape(B * T, HQ * HD)
    return matmul_res_ln(o, w_o, x2, ln_g, ln_b).reshape(B, T, D)


def _silu(x):
    return x * jax.nn.sigmoid(x)


def _conv_kernel(xp_ref, xc_ref, xn_ref, w_ref, o_ref, *, l2, scale):
    i = pl.program_id(1)
    tt = xc_ref.shape[1]
    pad = CONV_K // 2
    xp = jnp.where(i == 0, 0.0, xp_ref[0])
    xn = jnp.where(i == pl.num_programs(1) - 1, 0.0, xn_ref[0])
    xe = jnp.concatenate([xp, xc_ref[0], xn], axis=0)
    acc = xe[8 - pad:8 - pad + tt] * w_ref[0:1, :]
    for j in range(1, CONV_K):
        acc = acc + xe[8 - pad + j:8 - pad + j + tt] * w_ref[j:j + 1, :]
    y = _silu(acc)
    for h in range(y.shape[1] // DK):
        seg = y[:, h * DK:(h + 1) * DK]
        if l2:
            seg = seg * lax.rsqrt(jnp.sum(seg * seg, axis=-1, keepdims=True) + RMS_EPS) * scale
        o_ref[0, :, h * DK:(h + 1) * DK] = seg


def _conv_part(proj, conv_w, col0, ncols, l2, scale, tt=256, cb=512):
    B, T, _ = proj.shape
    tt = min(tt, T)
    assert T % tt == 0 and col0 % cb == 0 and ncols % cb == 0
    c0 = col0 // cb
    r8 = tt // 8
    return pl.pallas_call(
        functools.partial(_conv_kernel, l2=l2, scale=scale),
        grid=(B, T // tt, ncols // cb),
        in_specs=[
            pl.BlockSpec((1, 8, cb), lambda b, i, c: (b, jnp.maximum(i * r8 - 1, 0), c0 + c)),
            pl.BlockSpec((1, tt, cb), lambda b, i, c: (b, i, c0 + c)),
            pl.BlockSpec((1, 8, cb), lambda b, i, c: (b, jnp.minimum((i + 1) * r8, T // 8 - 1), c0 + c)),
            pl.BlockSpec((CONV_K, cb), lambda b, i, c: (0, c0 + c)),
        ],
        out_specs=pl.BlockSpec((1, tt, cb), lambda b, i, c: (b, i, c)),
        out_shape=jax.ShapeDtypeStruct((B, T, ncols), f32),
        compiler_params=_params("parallel", "parallel", "parallel"),
        name="dn_conv",
    )(proj, proj, proj, conv_w)


def conv_qkv(proj, conv_w):
    q = _conv_part(proj, conv_w, 0, QK_DIM, True, DK ** -0.5)
    k = _conv_part(proj, conv_w, QK_DIM, QK_DIM, True, 1.0)
    v = _conv_part(proj, conv_w, 2 * QK_DIM, V_DIM, False, 1.0)
    return q, k, v


def _gates_kernel(ab_ref, alog_ref, dtb_ref, o_ref):
    x = ab_ref[0]
    tt = x.shape[0]
    g = -jnp.exp(alog_ref[...]) * jax.nn.softplus(x + dtb_ref[...])
    beta = jax.nn.sigmoid(x)
    ri = lax.broadcasted_iota(i32, (tt, tt), 0)
    ci = lax.broadcasted_iota(i32, (tt, tt), 1)
    same = (ri // CHUNK) == (ci // CHUNK)
    lp = jnp.where(same & (ci <= ri), 1.0, 0.0).astype(f32)
    ls = jnp.where(same & (ci >= ri), 1.0, 0.0).astype(f32)
    gp = jnp.dot(lp, g, precision=lax.Precision.HIGHEST, preferred_element_type=f32)
    gs = jnp.dot(ls, g, precision=lax.Precision.HIGHEST, preferred_element_type=f32)
    lane = lax.broadcasted_iota(i32, x.shape, 1)
    cum = jnp.where(lane < 2 * HV, gp, gs)
    o_ref[0] = jnp.where((lane % (2 * HV)) < HV, cum, beta)


def gates(ab, a_log, dt_bias, tt=256):
    B, T, W = ab.shape
    tt = min(tt, T)
    z = jnp.zeros((2, HV), f32)
    alog = jnp.concatenate([a_log.astype(f32), z], axis=1).reshape(1, W)
    dtb = jnp.concatenate([dt_bias.astype(f32), z], axis=1).reshape(1, W)
    return pl.pallas_call(
        _gates_kernel,
        grid=(B, T // tt),
        in_specs=[pl.BlockSpec((1, tt, W), lambda b, i: (b, i, 0)), pl.BlockSpec((1, W), lambda b, i: (0, 0)),
                  pl.BlockSpec((1, W), lambda b, i: (0, 0))],
        out_specs=pl.BlockSpec((1, tt, W), lambda b, i: (b, i, 0)),
        out_shape=jax.ShapeDtypeStruct((B, T, W), f32),
        compiler_params=_params("parallel", "parallel"),
        name="dn_gates",
    )(ab, alog, dtb)


def _mxu_dot(a, b):
    return jnp.dot(a.astype(MXU), b.astype(MXU), preferred_element_type=f32)


def _unit_tri_inverse(L):
    c = L.shape[0]
    eye = (lax.broadcasted_iota(i32, (c, c), 0) == lax.broadcasted_iota(i32, (c, c), 1)).astype(f32)
    m = -L
    p = eye + m
    for _ in range(int(math.log2(c)) - 1):
        m = _mxu_dot(m, m)
        p = p + _mxu_dot(p, m)
    return p


def _delta_chunk(q_ref, k_ref, v_ref, gt_ref, o_ref, s_ref, rs, hk, reverse):
    C = CHUNK
    kc = k_ref[0, rs, :]
    qc = q_ref[0, rs, :]
    gram = lax.dot_general(jnp.concatenate([kc, qc], axis=0).astype(MXU), kc.astype(MXU), (((1,), (1,)), ((), ())),
                           preferred_element_type=f32)
    kk, qk = gram[:C], gram[C:]
    kT = kc.T
    gtc = gt_ref[0, rs, :]
    lane = lax.broadcasted_iota(i32, gtc.shape, 1)
    ri = lax.broadcasted_iota(i32, (C, C), 0)
    ci = lax.broadcasted_iota(i32, (C, C), 1)
    incl = (ri <= ci) if reverse else (ri >= ci)
    strict = (ri < ci) if reverse else (ri > ci)
    for j in range(HV // HK):
        lane_g = (2 * HV if reverse else 0) + (HV // HK) * hk + j
        gcol = jnp.sum(jnp.where(lane == lane_g, gtc, 0.0), axis=1, keepdims=True)
        bcol = jnp.sum(jnp.where(lane == lane_g + HV, gtc, 0.0), axis=1, keepdims=True)
        grow = jnp.sum(jnp.where(ri == ci, gcol, 0.0), axis=0, keepdims=True)
        glast = gcol[0:1, :] if reverse else gcol[C - 1:C, :]
        decay = jnp.exp(jnp.where(incl, gcol - grow, -jnp.inf))
        tinv = _unit_tri_inverse(jnp.where(strict, bcol * kk * decay, 0.0))
        vc = v_ref[0, rs, j * DV:(j + 1) * DV]
        eg = jnp.exp(gcol)
        sol = _mxu_dot(tinv, jnp.concatenate([vc * bcol, kc * (bcol * eg)], axis=1))
        u, w = sol[:, :DV], sol[:, DV:]
        ci_s = (2 if reverse else 0) + j
        s_old = s_ref[ci_s]
        wq = _mxu_dot(jnp.concatenate([w, qc * eg], axis=0), s_old)
        v_new = u - wq[:C]
        ik = _mxu_dot(jnp.concatenate([qk * decay, kT * jnp.exp(glast - grow)], axis=0), v_new)
        o_ref[0, rs, j * DV:(j + 1) * DV] = wq[C:] + ik[:C]
        s_ref[ci_s] = s_old * jnp.exp(glast) + ik[C:]


def _delta_kernel(qf, kf, vf, gf, qb, kb, vb, gb, of, ob, s_ref):
    hk = pl.program_id(1)

    @pl.when(pl.program_id(2) == 0)
    def _():
        s_ref[...] = jnp.zeros_like(s_ref)

    nc = qf.shape[1] // CHUNK
    for c in range(nc):
        _delta_chunk(qf, kf, vf, gf, of, s_ref, slice(c * CHUNK, (c + 1) * CHUNK), hk, False)
        cb = nc - 1 - c
        _delta_chunk(qb, kb, vb, gb, ob, s_ref, slice(cb * CHUNK, (cb + 1) * CHUNK), hk, True)


def delta_rule(q, k, v, gt, tb=256):
    B, T, _ = q.shape
    tb = min(tb, T)
    nT = T // tb
    rep = HV // HK
    fwd = lambda b, h, i: (b, i, h)
    bwd = lambda b, h, i: (b, nT - 1 - i, h)
    fwd0 = lambda b, h, i: (b, i, 0)
    bwd0 = lambda b, h, i: (b, nT - 1 - i, 0)
    qk_spec = lambda m: pl.BlockSpec((1, tb, DK), m)
    v_spec = lambda m: pl.BlockSpec((1, tb, rep * DV), m)
    g_spec = lambda m: pl.BlockSpec((1, tb, gt.shape[2]), m)
    o_shape = jax.ShapeDtypeStruct((B, T, V_DIM), f32)
    return pl.pallas_call(
        _delta_kernel,
        grid=(B, HK, nT),
        in_specs=[qk_spec(fwd), qk_spec(fwd), v_spec(fwd), g_spec(fwd0), qk_spec(bwd), qk_spec(bwd), v_spec(bwd), g_spec(bwd0)],
        out_specs=[v_spec(fwd), v_spec(bwd)],
        out_shape=[o_shape, o_shape],
        scratch_shapes=[pltpu.VMEM((2 * rep, DK, DV), f32)],
        compiler_params=_params("parallel", "parallel", "arbitrary"),
        name="dn_delta",
    )(q, k, v, gt, q, k, v, gt)


def _gnorm_kernel(of_ref, ob_ref, z_ref, nw_ref, o_ref):
    o = of_ref[...] + ob_ref[...]
    z = z_ref[...]
    for h in range(o.shape[1] // DV):
        hs = slice(h * DV, (h + 1) * DV)
        seg = o[:, hs]
        y = seg * lax.rsqrt(jnp.mean(seg * seg, axis=-1, keepdims=True) + RMS_EPS) * nw_ref[...]
        o_ref[:, hs] = (y * _silu(z[:, hs])).astype(o_ref.dtype)


def gated_norm(of, ob, z, norm_w, tt=256, cb=1024):
    M, W = of.shape
    tt = min(tt, M)
    spec = pl.BlockSpec((tt, cb), lambda i, c: (i, c))
    return pl.pallas_call(
        _gnorm_kernel,
        grid=(M // tt, W // cb),
        in_specs=[spec, spec, spec, pl.BlockSpec((1, DV), lambda i, c: (0, 0))],
        out_specs=spec,
        out_shape=jax.ShapeDtypeStruct((M, W), MXU),
        compiler_params=_params("parallel", "parallel"),
        name="dn_gated_norm",
    )(of, ob, z, norm_w.reshape(1, DV).astype(f32))


def dn_layer(x, w_conv, w_z, w_ab, conv_w, a_log, dt_bias, norm_w, w_out, ln_g, ln_b):
    B, T, D = x.shape
    x2 = x.reshape(B * T, D)
    pc = matmul(x2, w_conv).reshape(B, T, CONV_DIM)
    z = matmul(x2, w_z)
    ab = matmul(x2, w_ab).reshape(B, T, -1)
    q, k, v = conv_qkv(pc, conv_w)
    gt = gates(ab, a_log, dt_bias)
    of, ob = delta_rule(q, k, v, gt)
    o = gated_norm(of.reshape(B * T, V_DIM), ob.reshape(B * T, V_DIM), z, norm_w)
    return matmul_res_ln(o, w_out, x2, ln_g, ln_b).reshape(B, T, D)


EXACT = jnp.bfloat16


def _split3(x):
    hi = x.astype(MXU)
    return hi, (x - hi.astype(f32)).astype(MXU)


def _router_kernel(x_ref, w_ref, atok_ref, a2_ref):
    xh, xl = _split3(x_ref[...])
    wh, wl = _split3(w_ref[...])
    dot = functools.partial(jnp.dot, preferred_element_type=f32)
    logits = dot(xh, wh) + (dot(xl, wh) + dot(xh, wl))
    lane = lax.broadcasted_iota(i32, logits.shape, 1)
    lm = jnp.where(lane < N_EXPERTS, logits, -jnp.inf)
    e = jnp.exp(lm - jnp.max(lm, axis=-1, keepdims=True))
    aff = e / jnp.sum(e, axis=-1, keepdims=True)
    atok_ref[...] = aff
    for j in range(aff.shape[0] // LANES):
        bt = aff[j * LANES:(j + 1) * LANES, :].T
        for ex in range(N_EXPERTS):
            a2_ref[j:j + 1, ex * LANES:(ex + 1) * LANES] = bt[ex:ex + 1, :]


def router(x2, w_router, tm=1024):
    n, D = x2.shape
    tm = min(tm, n)
    w = jnp.zeros((D, LANES), f32).at[:, :N_EXPERTS].set(w_router.astype(f32))
    return pl.pallas_call(
        _router_kernel,
        grid=(n // tm,),
        in_specs=[pl.BlockSpec((tm, D), lambda i: (i, 0)), pl.BlockSpec((D, LANES), lambda i: (0, 0))],
        out_specs=[pl.BlockSpec((tm, LANES), lambda i: (i, 0)), pl.BlockSpec((tm // LANES, N_EXPERTS * LANES), lambda i: (i, 0))],
        out_shape=[jax.ShapeDtypeStruct((n, LANES), f32), jax.ShapeDtypeStruct((n // LANES, N_EXPERTS * LANES), f32)],
        compiler_params=_params("parallel"),
        name="moe_router",
    )(x2, w)


def _per_expert(row, op):
    parts = []
    for ex in range(N_EXPERTS):
        s = op(row[:, ex * LANES:(ex + 1) * LANES], axis=1, keepdims=True)
        parts.append(jnp.broadcast_to(s, (1, LANES)))
    return jnp.concatenate(parts, axis=1)


def _select_kernel(a_ref, idx_ref, dst_ref, tab_ref, kmax_ref, *, cap, n):
    R = a_ref.shape[0]
    W = N_EXPERTS * LANES
    bits = pltpu.bitcast(a_ref[...], i32)
    capf = jnp.float32(cap)

    def count(mask):
        return _per_expert(jnp.sum(jnp.where(mask, 1.0, 0.0), axis=0, keepdims=True), jnp.sum)

    def bit_step(t, ans):
        cand = ans | lax.shift_left(jnp.int32(1), 29 - t)
        return jnp.where(count(bits >= cand) >= capf, cand, ans)

    thr = lax.fori_loop(0, 30, bit_step, jnp.zeros((1, W), i32))
    gt = bits > thr
    eq = bits == thr
    need = capf - count(gt)

    lane_r = lax.broadcasted_iota(i32, (LANES, LANES), 0)
    lane_c = lax.broadcasted_iota(i32, (LANES, LANES), 1)
    upper = jnp.where(lane_r <= lane_c, 1.0, 0.0).astype(EXACT)
    row_r = lax.broadcasted_iota(i32, (R, R), 0)
    row_c = lax.broadcasted_iota(i32, (R, R), 1)
    below = jnp.where(row_c < row_r, 1.0, 0.0).astype(EXACT)

    def prefix(slab):
        lc = jnp.dot(slab.astype(EXACT), upper, preferred_element_type=f32)
        tot = jnp.broadcast_to(lc[:, LANES - 1:LANES], (R, LANES))
        off = jnp.dot(below, tot.astype(EXACT), preferred_element_type=f32)
        return lc, off, tot

    tok = (lax.broadcasted_iota(i32, (R, LANES), 0) * LANES + lax.broadcasted_iota(i32, (R, LANES), 1))
    sub_r = lax.broadcasted_iota(i32, (R, LANES), 0).astype(f32)
    sub_l = lax.broadcasted_iota(i32, (LANES, LANES), 0).astype(f32)
    lane_p = lax.broadcasted_iota(i32, (1, LANES), 1)
    nsel = jnp.zeros((R, LANES), f32)
    cut_parts = []
    for ex in range(N_EXPERTS):
        es = slice(ex * LANES, (ex + 1) * LANES)
        eq_e = jnp.where(eq[:, es], 1.0, 0.0)
        lc, off, _ = prefix(eq_e)
        take = eq[:, es] & (lc + off <= need[:, es])
        cut_parts.append(jnp.broadcast_to(jnp.max(jnp.max(jnp.where(take, tok, -1), axis=0, keepdims=True), axis=1, keepdims=True), (1, LANES)))
        sel_e = jnp.where(gt[:, es] | take, 1.0, 0.0)
        lc, off, tot = prefix(sel_e)
        rank_e = nsel
        nsel = nsel + sel_e
        hi = jnp.floor(off * (1.0 / LANES))
        lo = off - hi * LANES
        table = jnp.concatenate([lc.T, rank_e.T, hi.T[0:8], lo.T[0:8]], axis=0).astype(EXACT)
        end = off + tot

        def chunk(c, carry):
            c = lax.convert_element_type(c, i32)
            p =(c * LANES + lane_p).astype(f32)
            jp = jnp.sum(jnp.where(end <= p, 1.0, 0.0), axis=0, keepdims=True)
            onehot = jnp.where(sub_r == jp, 1.0, 0.0).astype(EXACT)
            g = jnp.dot(table, onehot, preferred_element_type=f32)
            p_loc = p - (g[2 * LANES:2 * LANES + 1] * LANES + g[2 * LANES + 8:2 * LANES + 9])
            lp = jnp.sum(jnp.where(g[:LANES] <= p_loc, 1.0, 0.0), axis=0, keepdims=True)
            rk = jnp.sum(jnp.where(sub_l == lp, g[LANES:2 * LANES], 0.0), axis=0, keepdims=True)
            t = jp * LANES + lp
            col = pl.ds(pl.multiple_of(c * LANES, LANES), LANES)
            idx_ref[ex:ex + 1, col] = t.astype(i32)
            dst_ref[ex:ex + 1, col] = (rk * n + t).astype(i32)
            return carry

        lax.fori_loop(0, cap // LANES, chunk, 0)

    diag = lax.broadcasted_iota(i32, (N_EXPERTS, LANES), 0) == lax.broadcasted_iota(i32, (N_EXPERTS, LANES), 1)
    thr_rows = jnp.concatenate([thr[:, ex * LANES:(ex + 1) * LANES] for ex in range(N_EXPERTS)], axis=0)
    cut_rows = jnp.concatenate(cut_parts, axis=0)
    tab_ref[...] = jnp.zeros_like(tab_ref)
    tab_ref[0:1, :] = jnp.sum(jnp.where(diag, thr_rows, 0), axis=0, keepdims=True)
    tab_ref[1:2, :] = jnp.sum(jnp.where(diag, cut_rows, 0), axis=0, keepdims=True)
    kmax_ref[...] = jnp.broadcast_to(jnp.max(nsel, axis=1, keepdims=True), (R, LANES)).astype(i32)


def select(a2, cap):
    R = a2.shape[0]
    n = R * LANES
    assert cap % LANES == 0
    return pl.pallas_call(
        functools.partial(_select_kernel, cap=cap, n=n),
        out_shape=[jax.ShapeDtypeStruct((N_EXPERTS, cap), i32), jax.ShapeDtypeStruct((N_EXPERTS, cap), i32),
                   jax.ShapeDtypeStruct((8, LANES), i32), jax.ShapeDtypeStruct((R, LANES), i32)],
        compiler_params=pltpu.CompilerParams(vmem_limit_bytes=V7X_VMEM_LIMIT),
        name="moe_select",
    )(a2)


def _ffn_kernel(idx_ref, dst_ref, x_hbm, wg_ref, wu_ref, wd_ref, y_hbm, xbuf, ybuf, gsem, ssem):
    tm = xbuf.shape[0]

    def gather(r):
        return pltpu.make_async_copy(x_hbm.at[pl.ds(idx_ref[0, 0, r], 1), :], xbuf.at[pl.ds(r, 1), :], gsem)

    def scatter(r):
        return pltpu.make_async_copy(ybuf.at[pl.ds(r, 1), :], y_hbm.at[pl.ds(dst_ref[0, 0, r], 1), :], ssem)

    def each_row(fn):
        def body(r, c):
            fn(r)
            return c
        lax.fori_loop(0, tm, body, 0)

    each_row(lambda r: gather(r).start())
    each_row(lambda r: gather(r).wait())
    xe = xbuf[...].astype(MXU)
    gate = jnp.dot(xe, wg_ref[...], preferred_element_type=f32)
    up = jnp.dot(xe, wu_ref[...], preferred_element_type=f32)
    h = (_silu(gate) * up).astype(MXU)
    ybuf[...] = jnp.dot(h, wd_ref[...], preferred_element_type=f32)
    each_row(lambda r: scatter(r).start())
    each_row(lambda r: scatter(r).wait())


def moe_ffn(x2, idx, dst, w_gate, w_up, w_down, tm=256):
    n, D = x2.shape
    E, cap = idx.shape
    F = w_gate.shape[2]
    tm = min(tm, cap)
    nc = cap // tm
    ispec = pl.BlockSpec((1, 1, tm), lambda e, c: (e * nc + c, 0, 0), memory_space=pltpu.SMEM)
    return pl.pallas_call(
        _ffn_kernel,
        grid=(E, nc),
        in_specs=[
            ispec, ispec,
            pl.BlockSpec(memory_space=pl.ANY),
            pl.BlockSpec((None, D, F), lambda e, c: (e, 0, 0)),
            pl.BlockSpec((None, D, F), lambda e, c: (e, 0, 0)),
            pl.BlockSpec((None, F, D), lambda e, c: (e, 0, 0)),
        ],
        out_specs=pl.BlockSpec(memory_space=pl.ANY),
        out_shape=jax.ShapeDtypeStruct((E * n, D), f32),
        scratch_shapes=[pltpu.VMEM((tm, D), f32), pltpu.VMEM((tm, D), f32), pltpu.SemaphoreType.DMA, pltpu.SemaphoreType.DMA],
        compiler_params=_params("arbitrary", "arbitrary"),
        name="moe_ffn",
    )(idx.reshape(E * nc, 1, tm), dst.reshape(E * nc, 1, tm), x2, w_gate, w_up, w_down)


def _combine_kernel(kmax_ref, y_hbm, atok_ref, tab_ref, x_ref, g_ref, b_ref, o_ref, ybuf, sem):
    i = pl.program_id(0)
    tt = x_ref.shape[0]
    km = kmax_ref[i]
    aff = atok_ref[...]
    bits = pltpu.bitcast(aff, i32)
    lane = lax.broadcasted_iota(i32, aff.shape, 1)
    tok = i * tt + lax.broadcasted_iota(i32, aff.shape, 0)
    thr, cut = tab_ref[0:1, :], tab_ref[1:2, :]
    sel = (lane < N_EXPERTS) & ((bits > thr) | ((bits == thr) & (tok <= cut)))
    before = jnp.where(lax.broadcasted_iota(i32, (LANES, LANES), 0) < lax.broadcasted_iota(i32, (LANES, LANES), 1), 1.0, 0.0)
    rank = jnp.dot(jnp.where(sel, 1.0, 0.0).astype(EXACT), before.astype(EXACT), preferred_element_type=f32)

    def fetch(k, slot):
        return pltpu.make_async_copy(y_hbm.at[k, pl.ds(i * tt, tt), :], ybuf.at[slot], sem.at[slot])

    @pl.when(km > 0)
    def _():
        fetch(0, 0).start()

    def slot_step(k, h):
        slot = k % 2

        @pl.when(k + 1 < km)
        def _():
            fetch(k + 1, 1 - slot).start()

        fetch(k, slot).wait()
        mine = sel & (rank == lax.convert_element_type(k, f32))
        gk = jnp.sum(jnp.where(mine, aff, 0.0), axis=1, keepdims=True)
        has = jnp.sum(jnp.where(mine, 1.0, 0.0), axis=1, keepdims=True) > 0.0
        return h + jnp.where(has, ybuf[slot], 0.0) * gk

    h = lax.fori_loop(0, km, slot_step, jnp.zeros(x_ref.shape, f32))
    o_ref[...] = _layer_norm_rows(DEEPNORM_ALPHA * x_ref[...] + h, g_ref[...], b_ref[...])


def moe_combine(y, atok, tab, kmax, x2, g, b, tt=LANES):
    n, D = x2.shape
    assert tt == LANES
    return pl.pallas_call(
        _combine_kernel,
        grid_spec=pltpu.PrefetchScalarGridSpec(
            num_scalar_prefetch=1,
            grid=(n // tt,),
            in_specs=[
                pl.BlockSpec(memory_space=pl.ANY),
                pl.BlockSpec((tt, LANES), lambda i, km: (i, 0)),
                pl.BlockSpec((8, LANES), lambda i, km: (0, 0)),
                pl.BlockSpec((tt, D), lambda i, km: (i, 0)),
                pl.BlockSpec((1, D), lambda i, km: (0, 0)),
                pl.BlockSpec((1, D), lambda i, km: (0, 0)),
            ],
            out_specs=pl.BlockSpec((tt, D), lambda i, km: (i, 0)),
            scratch_shapes=[pltpu.VMEM((2, tt, D), f32), pltpu.SemaphoreType.DMA((2,))],
        ),
        out_shape=jax.ShapeDtypeStruct((n, D), f32),
        compiler_params=_params("arbitrary"),
        name="moe_combine",
    )(kmax, y.reshape(N_EXPERTS, n, D), atok, tab, x2, g.reshape(1, D), b.reshape(1, D))


def moe_layer(x, w_router, w_gate, w_up, w_down, ln_g, ln_b):
    B, T, D = x.shape
    n = B * T
    x2 = x.reshape(n, D)
    cap = CAP_FACTOR * n // N_EXPERTS
    atok, a2 = router(x2, w_router)
    idx, dst, tab, kmax = select(a2, cap)
    y = moe_ffn(x2, idx, dst, w_gate, w_up, w_down)
    return moe_combine(y, atok, tab, kmax[:, 0], x2, ln_g, ln_b).reshape(B, T, D)


def _trunk(x, p):
    for i in range(DEPTH):
        j = i // 2
        if i % 2 == 0:
            x = attn_layer(x, p["attn_w_qkv"][j], p["attn_w_o"][j], p["attn_sink"][j], p["ln_mix_g"][i], p["ln_mix_b"][i])
        else:
            x = dn_layer(x, p["dn_w_conv"][j], p["dn_w_z"][j], p["dn_w_ab"][j], p["dn_conv_w"][j], p["dn_a_log"][j],
                         p["dn_dt_bias"][j], p["dn_norm_w"][j], p["dn_w_out"][j], p["ln_mix_g"][i], p["ln_mix_b"][i])
        x = moe_layer(x, p["moe_w_router"][i], p["moe_w_gate"][i], p["moe_w_up"][i], p["moe_w_down"][i],
                      p["ln_ffn_g"][i], p["ln_ffn_b"][i])
    return x


def kernel(x_prompt, x_sample, attn_w_qkv, attn_w_o, attn_sink, dn_w_in, dn_conv_w, dn_a_log, dn_dt_bias, dn_norm_w,
           dn_w_out, moe_w_router, moe_w_gate, moe_w_up, moe_w_down, ln_mix_g, ln_mix_b, ln_ffn_g, ln_ffn_b):
    c = lambda w: w.astype(MXU)
    p = dict(
        attn_w_qkv=c(attn_w_qkv), attn_w_o=c(attn_w_o), attn_sink=attn_sink,
        dn_w_conv=c(dn_w_in[:, :, :CONV_DIM]), dn_w_z=c(dn_w_in[:, :, CONV_DIM:CONV_DIM + V_DIM]),
        dn_w_ab=c(dn_w_in[:, :, CONV_DIM + V_DIM:]), dn_conv_w=dn_conv_w, dn_a_log=dn_a_log, dn_dt_bias=dn_dt_bias,
        dn_norm_w=dn_norm_w, dn_w_out=c(dn_w_out), moe_w_router=moe_w_router, moe_w_gate=c(moe_w_gate),
        moe_w_up=c(moe_w_up), moe_w_down=c(moe_w_down), ln_mix_g=ln_mix_g, ln_mix_b=ln_mix_b, ln_ffn_g=ln_ffn_g,
        ln_ffn_b=ln_ffn_b)
    return _trunk(x_prompt, p), _trunk(x_sample, p)
```

```python
import functools
import math

import jax
import jax.numpy as jnp
from jax import lax
from jax.experimental import pallas as pl
from jax.experimental.pallas import tpu as pltpu

f32, i32 = jnp.float32, jnp.int32
MXU = jnp.bfloat16

D_MODEL = 2048
DEPTH = 4
HQ, HKV, HD = 16, 4, 128
GROUP = HQ // HKV
BLOCK = 128
ROT_DIM = HD // 4
ROPE_THETA = 500000.0
HK, HV, DK, DV = 16, 32, 128, 128
QK_DIM, V_DIM = HK * DK, HV * DV
CONV_DIM = 2 * QK_DIM + V_DIM
CONV_K = 5
CHUNK = 64
N_EXPERTS = 16
D_EXPERT = 1024
CAP_FACTOR = 2
DEEPNORM_ALPHA = (2.0 * DEPTH) ** 0.25
LN_EPS = 1e-5
RMS_EPS = 1e-6

LANES = 128
V7X_VMEM_LIMIT = 56 * 1024 * 1024


def _params(*sem):
    return pltpu.CompilerParams(dimension_semantics=sem, vmem_limit_bytes=V7X_VMEM_LIMIT)


def _mm_kernel(a_ref, w_ref, o_ref):
    o_ref[...] = jnp.dot(a_ref[...].astype(MXU), w_ref[...], preferred_element_type=f32).astype(o_ref.dtype)


def matmul(a, w, out_dtype=f32, tm=512, tn=1024):
    M, K = a.shape
    N = w.shape[1]
    tm, tn = min(tm, M), min(tn, N)
    assert M % tm == 0 and N % tn == 0
    return pl.pallas_call(
        _mm_kernel,
        grid=(N // tn, M // tm),
        in_specs=[pl.BlockSpec((tm, K), lambda j, i: (i, 0)), pl.BlockSpec((K, tn), lambda j, i: (0, j))],
        out_specs=pl.BlockSpec((tm, tn), lambda j, i: (i, j)),
        out_shape=jax.ShapeDtypeStruct((M, N), out_dtype),
        compiler_params=_params("parallel", "parallel"),
        name="matmul",
    )(a, w)


def _layer_norm_rows(y, g, b):
    mu = jnp.mean(y, axis=-1, keepdims=True)
    yc = y - mu
    var = jnp.mean(yc * yc, axis=-1, keepdims=True)
    return yc * lax.rsqrt(var + LN_EPS) * g + b


def _mm_ln_kernel(a_ref, w_ref, x_ref, g_ref, b_ref, o_ref):
    h = jnp.dot(a_ref[...].astype(MXU), w_ref[...], preferred_element_type=f32)
    o_ref[...] = _layer_norm_rows(DEEPNORM_ALPHA * x_ref[...] + h, g_ref[...], b_ref[...])


def matmul_res_ln(a, w, x, g, b, tm=256):
    M, K = a.shape
    D = w.shape[1]
    tm = min(tm, M)
    assert M % tm == 0
    return pl.pallas_call(
        _mm_ln_kernel,
        grid=(M // tm,),
        in_specs=[
            pl.BlockSpec((tm, K), lambda i: (i, 0)),
            pl.BlockSpec((K, D), lambda i: (0, 0), pipeline_mode=pl.Buffered(1)),
            pl.BlockSpec((tm, D), lambda i: (i, 0)),
            pl.BlockSpec((1, D), lambda i: (0, 0)),
            pl.BlockSpec((1, D), lambda i: (0, 0)),
        ],
        out_specs=pl.BlockSpec((tm, D), lambda i: (i, 0)),
        out_shape=jax.ShapeDtypeStruct((M, D), f32),
        compiler_params=_params("parallel"),
        name="matmul_res_ln",
    )(a, w, x, g.reshape(1, D), b.reshape(1, D))


def rope_table(T):
    half = ROT_DIM // 2
    inv = ROPE_THETA ** (-jnp.arange(0, ROT_DIM, 2, dtype=f32) / ROT_DIM)
    ang = jnp.arange(T, dtype=f32)[:, None] * inv[None, :]
    c, s = jnp.cos(ang), jnp.sin(ang)
    z = jnp.zeros((T, HD - ROT_DIM), f32)
    zh = jnp.zeros((T, half), f32)
    return jnp.concatenate([c, c, z + 1.0, -s, zh, z, zh, s, z], axis=1)


def _rope(t, tab):
    return t * tab[:, :HD] + pltpu.roll(t, HD - ROT_DIM // 2, 1) * tab[:, HD:2 * HD] + pltpu.roll(t, ROT_DIM // 2, 1) * tab[:, 2 * HD:]


def _attn_kernel(sink_ref, q_ref, kp_ref, kc_ref, kn_ref, vp_ref, vc_ref, vn_ref, tp_ref, tc_ref, tn_ref, o_ref):
    n = pl.program_id(1)
    nb = pl.num_programs(1)
    ii = lax.broadcasted_iota(i32, (GROUP * BLOCK, 3 * BLOCK), 0) % BLOCK
    jj = lax.broadcasted_iota(i32, (GROUP * BLOCK, 3 * BLOCK), 1)
    lo = jnp.where(n == 0, BLOCK, 0)
    hi = jnp.where(n == nb - 1, 2 * BLOCK, 3 * BLOCK)
    valid = (jj >= ii) & (jj <= ii + 2 * BLOCK) & (jj >= lo) & (jj < hi)
    tq = tc_ref[...]
    tabs = (tp_ref[...], tq, tn_ref[...])
    for h in range(HKV):
        hs = slice(h * HD, (h + 1) * HD)
        kb = jnp.concatenate([_rope(r[0, :, hs], t) for r, t in zip((kp_ref, kc_ref, kn_ref), tabs)], axis=0).astype(MXU)
        vb = jnp.concatenate([r[0, :, hs] for r in (vp_ref, vc_ref, vn_ref)], axis=0).astype(MXU)
        qh = jnp.concatenate([_rope(q_ref[0, :, (h * GROUP + g) * HD:(h * GROUP + g + 1) * HD], tq) for g in range(GROUP)],
                             axis=0).astype(MXU)
        s = lax.dot_general(qh, kb, (((1,), (1,)), ((), ())), preferred_element_type=f32) * (HD ** -0.5)
        s = jnp.where(valid, s, -jnp.inf)
        sk = jnp.concatenate([jnp.full((BLOCK, 1), sink_ref[0, h * GROUP + g], f32) for g in range(GROUP)], axis=0)
        m = jnp.maximum(jnp.max(s, axis=-1, keepdims=True), sk)
        p = jnp.exp(s - m)
        denom = jnp.sum(p, axis=-1, keepdims=True) + jnp.exp(sk - m)
        p = (p / denom).astype(MXU)
        o = jnp.dot(p, vb, preferred_element_type=f32)
        for g in range(GROUP):
            o_ref[0, :, (h * GROUP + g) * HD:(h * GROUP + g + 1) * HD] = o[g * BLOCK:(g + 1) * BLOCK].astype(o_ref.dtype)


def attention(qkv, sink):
    B, T, _ = qkv.shape
    out_dtype = MXU
    nb = T // BLOCK
    tab = rope_table(T)
    kcol, vcol = HQ // HKV, HQ // HKV + 1
    kvw = HKV * HD
    prev = lambda n: jnp.maximum(n - 1, 0)
    nxt = lambda n: jnp.minimum(n + 1, nb - 1)
    return pl.pallas_call(
        _attn_kernel,
        grid=(B, nb),
        in_specs=[
            pl.BlockSpec(memory_space=pltpu.SMEM),
            pl.BlockSpec((1, BLOCK, HQ * HD), lambda b, n: (b, n, 0)),
            pl.BlockSpec((1, BLOCK, kvw), lambda b, n: (b, prev(n), kcol)),
            pl.BlockSpec((1, BLOCK, kvw), lambda b, n: (b, n, kcol)),
            pl.BlockSpec((1, BLOCK, kvw), lambda b, n: (b, nxt(n), kcol)),
            pl.BlockSpec((1, BLOCK, kvw), lambda b, n: (b, prev(n), vcol)),
            pl.BlockSpec((1, BLOCK, kvw), lambda b, n: (b, n, vcol)),
            pl.BlockSpec((1, BLOCK, kvw), lambda b, n: (b, nxt(n), vcol)),
            pl.BlockSpec((BLOCK, 3 * HD), lambda b, n: (prev(n), 0)),
            pl.BlockSpec((BLOCK, 3 * HD), lambda b, n: (n, 0)),
            pl.BlockSpec((BLOCK, 3 * HD), lambda b, n: (nxt(n), 0)),
        ],
        out_specs=pl.BlockSpec((1, BLOCK, HQ * HD), lambda b, n: (b, n, 0)),
        out_shape=jax.ShapeDtypeStruct((B, T, HQ * HD), out_dtype),
        compiler_params=_params("parallel", "parallel"),
        name="attention",
    )(sink.reshape(1, HQ).astype(f32), qkv, qkv, qkv, qkv, qkv, qkv, qkv, tab, tab, tab)


def attn_layer(x, w_qkv, w_o, sink, ln_g, ln_b):
    B, T, D = x.shape
    x2 = x.reshape(B * T, D)
    qkv = matmul(x2, w_qkv).reshape(B, T, -1)
    o = attention(qkv, sink).reshape(B * T, HQ * HD)
    return matmul_res_ln(o, w_o, x2, ln_g, ln_b).reshape(B, T, D)


def _silu(x):
    return x * jax.nn.sigmoid(x)


def _conv_kernel(xp_ref, xc_ref, xn_ref, w_ref, o_ref, *, l2, scale):
    i = pl.program_id(1)
    tt = xc_ref.shape[1]
    pad = CONV_K // 2
    xp = jnp.where(i == 0, 0.0, xp_ref[0])
    xn = jnp.where(i == pl.num_programs(1) - 1, 0.0, xn_ref[0])
    xe = jnp.concatenate([xp, xc_ref[0], xn], axis=0)
    acc = xe[8 - pad:8 - pad + tt] * w_ref[0:1, :]
    for j in range(1, CONV_K):
        acc = acc + xe[8 - pad + j:8 - pad + j + tt] * w_ref[j:j + 1, :]
    y = _silu(acc)
    for h in range(y.shape[1] // DK):
        seg = y[:, h * DK:(h + 1) * DK]
        if l2:
            seg = seg * lax.rsqrt(jnp.sum(seg * seg, axis=-1, keepdims=True) + RMS_EPS) * scale
        o_ref[0, :, h * DK:(h + 1) * DK] = seg


def _conv_part(proj, conv_w, col0, ncols, l2, scale, tt=512, cb=512):
    B, T, _ = proj.shape
    tt = min(tt, T)
    assert T % tt == 0 and col0 % cb == 0 and ncols % cb == 0
    c0 = col0 // cb
    r8 = tt // 8
    return pl.pallas_call(
        functools.partial(_conv_kernel, l2=l2, scale=scale),
        grid=(B, T // tt, ncols // cb),
        in_specs=[
            pl.BlockSpec((1, 8, cb), lambda b, i, c: (b, jnp.maximum(i * r8 - 1, 0), c0 + c)),
            pl.BlockSpec((1, tt, cb), lambda b, i, c: (b, i, c0 + c)),
            pl.BlockSpec((1, 8, cb), lambda b, i, c: (b, jnp.minimum((i + 1) * r8, T // 8 - 1), c0 + c)),
            pl.BlockSpec((CONV_K, cb), lambda b, i, c: (0, c0 + c)),
        ],
        out_specs=pl.BlockSpec((1, tt, cb), lambda b, i, c: (b, i, c)),
        out_shape=jax.ShapeDtypeStruct((B, T, ncols), f32),
        compiler_params=_params("parallel", "parallel", "parallel"),
        name="dn_conv",
    )(proj, proj, proj, conv_w)


def conv_qkv(proj, conv_w):
    q = _conv_part(proj, conv_w, 0, QK_DIM, True, DK ** -0.5)
    k = _conv_part(proj, conv_w, QK_DIM, QK_DIM, True, 1.0)
    v = _conv_part(proj, conv_w, 2 * QK_DIM, V_DIM, False, 1.0)
    return q, k, v


def _gates_kernel(ab_ref, alog_ref, dtb_ref, o_ref):
    x = ab_ref[0]
    tt = x.shape[0]
    g = -jnp.exp(alog_ref[...]) * jax.nn.softplus(x + dtb_ref[...])
    beta = jax.nn.sigmoid(x)
    ri = lax.broadcasted_iota(i32, (tt, tt), 0)
    ci = lax.broadcasted_iota(i32, (tt, tt), 1)
    same = (ri // CHUNK) == (ci // CHUNK)
    lp = jnp.where(same & (ci <= ri), 1.0, 0.0).astype(f32)
    ls = jnp.where(same & (ci >= ri), 1.0, 0.0).astype(f32)
    gp = jnp.dot(lp, g, precision=lax.Precision.HIGHEST, preferred_element_type=f32)
    gs = jnp.dot(ls, g, precision=lax.Precision.HIGHEST, preferred_element_type=f32)
    lane = lax.broadcasted_iota(i32, x.shape, 1)
    cum = jnp.where(lane < 2 * HV, gp, gs)
    o_ref[0] = jnp.where((lane % (2 * HV)) < HV, cum, beta)


def gates(ab, a_log, dt_bias, tt=256):
    B, T, W = ab.shape
    tt = min(tt, T)
    z = jnp.zeros((2, HV), f32)
    alog = jnp.concatenate([a_log.astype(f32), z], axis=1).reshape(1, W)
    dtb = jnp.concatenate([dt_bias.astype(f32), z], axis=1).reshape(1, W)
    return pl.pallas_call(
        _gates_kernel,
        grid=(B, T // tt),
        in_specs=[pl.BlockSpec((1, tt, W), lambda b, i: (b, i, 0)), pl.BlockSpec((1, W), lambda b, i: (0, 0)),
                  pl.BlockSpec((1, W), lambda b, i: (0, 0))],
        out_specs=pl.BlockSpec((1, tt, W), lambda b, i: (b, i, 0)),
        out_shape=jax.ShapeDtypeStruct((B, T, W), f32),
        compiler_params=_params("parallel", "parallel"),
        name="dn_gates",
    )(ab, alog, dtb)


def _bmm(a, b):
    return jnp.einsum("bik,bkj->bij", a.astype(MXU), b.astype(MXU), preferred_element_type=f32)


def _unit_tri_inverse(L):
    c = L.shape[-1]
    eye = (lax.broadcasted_iota(i32, (c, c), 0) == lax.broadcasted_iota(i32, (c, c), 1)).astype(f32)
    m = -L
    p = eye[None] + m
    for _ in range(int(math.log2(c)) - 1):
        m = _bmm(m, m)
        p = p + _bmm(p, m)
    return p


def _delta_kernel(qf, kf, vf, gf, qb, kb, vb, gb, of, ob, s_ref, *, hb):
    C = CHUNK
    rep = HV // HK
    h0 = pl.program_id(1) * hb

    @pl.when(pl.program_id(2) == 0)
    def _():
        s_ref[...] = jnp.zeros_like(s_ref)

    tb = qf.shape[1]
    nc = tb // C
    ri = lax.broadcasted_iota(i32, (C, C), 0)
    ci = lax.broadcasted_iota(i32, (C, C), 1)
    eye = (ri == ci)[None]
    Ls, rhs, lhs2, qeg, egl = [], [], [], [], []
    for reverse, (q_ref, k_ref, v_ref, g_ref) in enumerate(((qf, kf, vf, gf), (qb, kb, vb, gb))):
        incl = ((ri <= ci) if reverse else (ri >= ci))[None]
        strict = ((ri < ci) if reverse else (ri > ci))[None]
        gt = g_ref[0]
        lane = lax.broadcasted_iota(i32, gt.shape, 1)
        for h in range(hb):
            k3 = k_ref[0, :, h * DK:(h + 1) * DK].reshape(nc, C, DK)
            q3 = q_ref[0, :, h * DK:(h + 1) * DK].reshape(nc, C, DK)
            gram = jnp.einsum("cid,cjd->cij", jnp.concatenate([k3, q3], axis=1).astype(MXU), k3.astype(MXU),
                              preferred_element_type=f32)
            kk, qk = gram[:, :C], gram[:, C:]
            kT3 = jnp.concatenate([k3[c].T[None] for c in range(nc)], axis=0)
            for j in range(rep):
                lane_g = (2 * HV if reverse else 0) + rep * (h0 + h) + j
                gcol = jnp.sum(jnp.where(lane == lane_g, gt, 0.0), axis=1, keepdims=True).reshape(nc, C, 1)
                bcol = jnp.sum(jnp.where(lane == lane_g + HV, gt, 0.0), axis=1, keepdims=True).reshape(nc, C, 1)
                grow = jnp.sum(jnp.where(eye, gcol, 0.0), axis=1, keepdims=True)
                glast = gcol[:, 0:1, :] if reverse else gcol[:, C - 1:C, :]
                decay = jnp.exp(jnp.where(incl, gcol - grow, -jnp.inf))
                eg = jnp.exp(gcol)
                v3 = v_ref[0, :, (h * rep + j) * DV:(h * rep + j + 1) * DV].reshape(nc, C, DV)
                Ls.append(jnp.where(strict, bcol * kk * decay, 0.0))
                rhs.append(jnp.concatenate([v3 * bcol, k3 * (bcol * eg)], axis=2))
                lhs2.append(jnp.concatenate([kT3 * jnp.exp(glast - grow), qk * decay], axis=1))
                qeg.append(q3 * eg)
                egl.append(jnp.exp(glast))
    cat = lambda xs: jnp.concatenate(xs, axis=0)
    sol = _bmm(_unit_tri_inverse(cat(Ls)), cat(rhs))
    r2 = _bmm(cat(lhs2), sol)
    aq = jnp.concatenate([r2[:, :DK, DV:], cat(qeg) - r2[:, DK:, DV:]], axis=1).astype(MXU)
    bm, op, egl = r2[:, :DK, :DV], r2[:, DK:, :DV], cat(egl)
    nch = 2 * hb * rep
    S = [s_ref[ch] for ch in range(nch)]
    for step in range(nc):
        for ch in range(nch):
            reverse, col = divmod(ch, hb * rep)
            c = nc - 1 - step if reverse else step
            i = ch * nc + c
            r = jnp.dot(aq[i], S[ch].astype(MXU), preferred_element_type=f32)
            (ob if reverse else of)[0, c * C:(c + 1) * C, col * DV:(col + 1) * DV] = r[DK:] + op[i]
            S[ch] = S[ch] * egl[i] - r[:DK] + bm[i]
    for ch in range(nch):
        s_ref[ch] = S[ch]


def delta_rule(q, k, v, gt, tb=256, hb=2):
    B, T, _ = q.shape
    tb = min(tb, T)
    nT = T // tb
    rep = HV // HK
    fwd = lambda b, h, i: (b, i, h)
    bwd = lambda b, h, i: (b, nT - 1 - i, h)
    fwd0 = lambda b, h, i: (b, i, 0)
    bwd0 = lambda b, h, i: (b, nT - 1 - i, 0)
    qk_spec = lambda m: pl.BlockSpec((1, tb, hb * DK), m)
    v_spec = lambda m: pl.BlockSpec((1, tb, hb * rep * DV), m)
    g_spec = lambda m: pl.BlockSpec((1, tb, gt.shape[2]), m)
    o_shape = jax.ShapeDtypeStruct((B, T, V_DIM), f32)
    return pl.pallas_call(
        functools.partial(_delta_kernel, hb=hb),
        grid=(B, HK // hb, nT),
        in_specs=[qk_spec(fwd), qk_spec(fwd), v_spec(fwd), g_spec(fwd0), qk_spec(bwd), qk_spec(bwd), v_spec(bwd), g_spec(bwd0)],
        out_specs=[v_spec(fwd), v_spec(bwd)],
        out_shape=[o_shape, o_shape],
        scratch_shapes=[pltpu.VMEM((2 * hb * rep, DK, DV), f32)],
        compiler_params=_params("parallel", "parallel", "arbitrary"),
        name="dn_delta",
    )(q, k, v, gt, q, k, v, gt)


def _gnorm_kernel(of_ref, ob_ref, z_ref, nw_ref, o_ref):
    o = of_ref[...] + ob_ref[...]
    z = z_ref[...]
    for h in range(o.shape[1] // DV):
        hs = slice(h * DV, (h + 1) * DV)
        seg = o[:, hs]
        y = seg * lax.rsqrt(jnp.mean(seg * seg, axis=-1, keepdims=True) + RMS_EPS) * nw_ref[...]
        o_ref[:, hs] = (y * _silu(z[:, hs])).astype(o_ref.dtype)


def gated_norm(of, ob, z, norm_w, tt=256, cb=1024):
    M, W = of.shape
    tt = min(tt, M)
    spec = pl.BlockSpec((tt, cb), lambda i, c: (i, c))
    return pl.pallas_call(
        _gnorm_kernel,
        grid=(M // tt, W // cb),
        in_specs=[spec, spec, spec, pl.BlockSpec((1, DV), lambda i, c: (0, 0))],
        out_specs=spec,
        out_shape=jax.ShapeDtypeStruct((M, W), MXU),
        compiler_params=_params("parallel", "parallel"),
        name="dn_gated_norm",
    )(of, ob, z, norm_w.reshape(1, DV).astype(f32))


def dn_layer(x, w_conv, w_z, w_ab, conv_w, a_log, dt_bias, norm_w, w_out, ln_g, ln_b):
    B, T, D = x.shape
    x2 = x.reshape(B * T, D)
    pc = matmul(x2, w_conv).reshape(B, T, CONV_DIM)
    z = matmul(x2, w_z)
    ab = matmul(x2, w_ab).reshape(B, T, -1)
    q, k, v = conv_qkv(pc, conv_w)
    gt = gates(ab, a_log, dt_bias)
    of, ob = delta_rule(q, k, v, gt)
    o = gated_norm(of.reshape(B * T, V_DIM), ob.reshape(B * T, V_DIM), z, norm_w)
    return matmul_res_ln(o, w_out, x2, ln_g, ln_b).reshape(B, T, D)


EXACT = jnp.bfloat16


def _split3(x):
    hi = x.astype(MXU)
    return hi, (x - hi.astype(f32)).astype(MXU)


def _router_kernel(x_ref, w_ref, atok_ref, a2_ref):
    xh, xl = _split3(x_ref[...])
    wh, wl = _split3(w_ref[...])
    dot = functools.partial(jnp.dot, preferred_element_type=f32)
    logits = dot(xh, wh) + (dot(xl, wh) + dot(xh, wl))
    lane = lax.broadcasted_iota(i32, logits.shape, 1)
    lm = jnp.where(lane < N_EXPERTS, logits, -jnp.inf)
    e = jnp.exp(lm - jnp.max(lm, axis=-1, keepdims=True))
    aff = e / jnp.sum(e, axis=-1, keepdims=True)
    atok_ref[...] = aff
    for j in range(aff.shape[0] // LANES):
        bt = aff[j * LANES:(j + 1) * LANES, :].T
        for ex in range(N_EXPERTS):
            a2_ref[j:j + 1, ex * LANES:(ex + 1) * LANES] = bt[ex:ex + 1, :]


def router(x2, w_router, tm=1024):
    n, D = x2.shape
    tm = min(tm, n)
    w = jnp.zeros((D, LANES), f32).at[:, :N_EXPERTS].set(w_router.astype(f32))
    return pl.pallas_call(
        _router_kernel,
        grid=(n // tm,),
        in_specs=[pl.BlockSpec((tm, D), lambda i: (i, 0)), pl.BlockSpec((D, LANES), lambda i: (0, 0))],
        out_specs=[pl.BlockSpec((tm, LANES), lambda i: (i, 0)), pl.BlockSpec((tm // LANES, N_EXPERTS * LANES), lambda i: (i, 0))],
        out_shape=[jax.ShapeDtypeStruct((n, LANES), f32), jax.ShapeDtypeStruct((n // LANES, N_EXPERTS * LANES), f32)],
        compiler_params=_params("parallel"),
        name="moe_router",
    )(x2, w)


def _per_expert(row, op):
    parts = []
    for ex in range(N_EXPERTS):
        s = op(row[:, ex * LANES:(ex + 1) * LANES], axis=1, keepdims=True)
        parts.append(jnp.broadcast_to(s, (1, LANES)))
    return jnp.concatenate(parts, axis=1)


def _select_kernel(a_ref, idx_ref, dst_ref, tab_ref, kmax_ref, *, cap, n):
    R = a_ref.shape[0]
    W = N_EXPERTS * LANES
    bits = pltpu.bitcast(a_ref[...], i32)
    capf = jnp.float32(cap)

    def count(mask):
        return _per_expert(jnp.sum(jnp.where(mask, 1.0, 0.0), axis=0, keepdims=True), jnp.sum)

    def bit_step(t, ans):
        cand = ans | lax.shift_left(jnp.int32(1), 29 - t)
        return jnp.where(count(bits >= cand) >= capf, cand, ans)

    thr = lax.fori_loop(0, 30, bit_step, jnp.zeros((1, W), i32))
    gt = bits > thr
    eq = bits == thr
    need = capf - count(gt)

    lane_r = lax.broadcasted_iota(i32, (LANES, LANES), 0)
    lane_c = lax.broadcasted_iota(i32, (LANES, LANES), 1)
    upper = jnp.where(lane_r <= lane_c, 1.0, 0.0).astype(EXACT)
    row_r = lax.broadcasted_iota(i32, (R, R), 0)
    row_c = lax.broadcasted_iota(i32, (R, R), 1)
    below = jnp.where(row_c < row_r, 1.0, 0.0).astype(EXACT)

    def prefix(slab):
        lc = jnp.dot(slab.astype(EXACT), upper, preferred_element_type=f32)
        tot = jnp.broadcast_to(lc[:, LANES - 1:LANES], (R, LANES))
        off = jnp.dot(below, tot.astype(EXACT), preferred_element_type=f32)
        return lc, off, tot

    tok = (lax.broadcasted_iota(i32, (R, LANES), 0) * LANES + lax.broadcasted_iota(i32, (R, LANES), 1))
    sub_r = lax.broadcasted_iota(i32, (R, LANES), 0).astype(f32)
    sub_l = lax.broadcasted_iota(i32, (LANES, LANES), 0).astype(f32)
    lane_p = lax.broadcasted_iota(i32, (1, LANES), 1)
    nsel = jnp.zeros((R, LANES), f32)
    cut_parts = []
    for ex in range(N_EXPERTS):
        es = slice(ex * LANES, (ex + 1) * LANES)
        eq_e = jnp.where(eq[:, es], 1.0, 0.0)
        lc, off, _ = prefix(eq_e)
        take = eq[:, es] & (lc + off <= need[:, es])
        cut_parts.append(jnp.broadcast_to(jnp.max(jnp.max(jnp.where(take, tok, -1), axis=0, keepdims=True), axis=1, keepdims=True), (1, LANES)))
        sel_e = jnp.where(gt[:, es] | take, 1.0, 0.0)
        lc, off, tot = prefix(sel_e)
        rank_e = nsel
        nsel = nsel + sel_e
        hi = jnp.floor(off * (1.0 / LANES))
        lo = off - hi * LANES
        table = jnp.concatenate([lc.T, rank_e.T, hi.T[0:8], lo.T[0:8]], axis=0).astype(EXACT)
        end = off + tot

        def chunk(c, carry):
            c = lax.convert_element_type(c, i32)
            p =(c * LANES + lane_p).astype(f32)
            jp = jnp.sum(jnp.where(end <= p, 1.0, 0.0), axis=0, keepdims=True)
            onehot = jnp.where(sub_r == jp, 1.0, 0.0).astype(EXACT)
            g = jnp.dot(table, onehot, preferred_element_type=f32)
            p_loc = p - (g[2 * LANES:2 * LANES + 1] * LANES + g[2 * LANES + 8:2 * LANES + 9])
            lp = jnp.sum(jnp.where(g[:LANES] <= p_loc, 1.0, 0.0), axis=0, keepdims=True)
            rk = jnp.sum(jnp.where(sub_l == lp, g[LANES:2 * LANES], 0.0), axis=0, keepdims=True)
            t = jp * LANES + lp
            col = pl.ds(pl.multiple_of(c * LANES, LANES), LANES)
            idx_ref[ex:ex + 1, col] = t.astype(i32)
            dst_ref[ex:ex + 1, col] = (rk * n + t).astype(i32)
            return carry

        lax.fori_loop(0, cap // LANES, chunk, 0)

    diag = lax.broadcasted_iota(i32, (N_EXPERTS, LANES), 0) == lax.broadcasted_iota(i32, (N_EXPERTS, LANES), 1)
    thr_rows = jnp.concatenate([thr[:, ex * LANES:(ex + 1) * LANES] for ex in range(N_EXPERTS)], axis=0)
    cut_rows = jnp.concatenate(cut_parts, axis=0)
    tab_ref[...] = jnp.zeros_like(tab_ref)
    tab_ref[0:1, :] = jnp.sum(jnp.where(diag, thr_rows, 0), axis=0, keepdims=True)
    tab_ref[1:2, :] = jnp.sum(jnp.where(diag, cut_rows, 0), axis=0, keepdims=True)
    kmax_ref[...] = jnp.broadcast_to(jnp.max(nsel, axis=1, keepdims=True), (R, LANES)).astype(i32)


def select(a2, cap):
    R = a2.shape[0]
    n = R * LANES
    assert cap % LANES == 0
    return pl.pallas_call(
        functools.partial(_select_kernel, cap=cap, n=n),
        out_shape=[jax.ShapeDtypeStruct((N_EXPERTS, cap), i32), jax.ShapeDtypeStruct((N_EXPERTS, cap), i32),
                   jax.ShapeDtypeStruct((8, LANES), i32), jax.ShapeDtypeStruct((R, LANES), i32)],
        compiler_params=pltpu.CompilerParams(vmem_limit_bytes=V7X_VMEM_LIMIT),
        name="moe_select",
    )(a2)


ROW_DMA_UNROLL = 8


def _ffn_kernel(idx_ref, nidx_ref, dst_ref, x_hbm, wg_ref, wu_ref, wd_ref, y_hbm, xbuf, ybuf, gsem, ssem):
    tm = xbuf.shape[1]
    s = pl.program_id(0) * pl.num_programs(1) + pl.program_id(1)
    last = pl.num_programs(0) * pl.num_programs(1) - 1
    slot = s % 2

    def gather(ref, r, sl):
        return pltpu.make_async_copy(x_hbm.at[pl.ds(ref[0, 0, r], 1), :], xbuf.at[sl, pl.ds(r, 1), :], gsem.at[sl])

    def scatter(r, sl):
        return pltpu.make_async_copy(ybuf.at[sl, pl.ds(r, 1), :], y_hbm.at[pl.ds(dst_ref[0, 0, r], 1), :], ssem.at[sl])

    def each_row(fn):
        def body(r, c):
            fn(r)
            return c
        lax.fori_loop(0, tm, body, 0, unroll=ROW_DMA_UNROLL)

    @pl.when(s == 0)
    def _():
        each_row(lambda r: gather(idx_ref, r, 0).start())

    each_row(lambda r: gather(idx_ref, r, slot).wait())

    @pl.when(s >= 2)
    def _():
        each_row(lambda r: scatter(r, slot).wait())

    for r in range(tm):
        gather(nidx_ref, r, 1 - slot).start()
    xe = xbuf[slot].astype(MXU)
    gate = jnp.dot(xe, wg_ref[...], preferred_element_type=f32)
    up = jnp.dot(xe, wu_ref[...], preferred_element_type=f32)
    h = (_silu(gate) * up).astype(MXU)
    ybuf[slot] = jnp.dot(h, wd_ref[...], preferred_element_type=f32)
    for r in range(tm):
        scatter(r, slot).start()

    @pl.when(s == last)
    def _():
        each_row(lambda r: gather(nidx_ref, r, 1 - slot).wait())

        @pl.when(s >= 1)
        def _():
            each_row(lambda r: scatter(r, 1 - slot).wait())

        each_row(lambda r: scatter(r, slot).wait())


def moe_ffn(x2, idx, dst, w_gate, w_up, w_down, tm=256):
    n, D = x2.shape
    E, cap = idx.shape
    F = w_gate.shape[2]
    tm = min(tm, cap)
    nc = cap // tm
    ispec = pl.BlockSpec((1, 1, tm), lambda e, c: (e * nc + c, 0, 0), memory_space=pltpu.SMEM)
    nspec = pl.BlockSpec((1, 1, tm), lambda e, c: (jnp.minimum(e * nc + c + 1, E * nc - 1), 0, 0), memory_space=pltpu.SMEM)
    idx3 = idx.reshape(E * nc, 1, tm)
    return pl.pallas_call(
        _ffn_kernel,
        grid=(E, nc),
        in_specs=[
            ispec, nspec, ispec,
            pl.BlockSpec(memory_space=pl.ANY),
            pl.BlockSpec((None, D, F), lambda e, c: (e, 0, 0)),
            pl.BlockSpec((None, D, F), lambda e, c: (e, 0, 0)),
            pl.BlockSpec((None, F, D), lambda e, c: (e, 0, 0)),
        ],
        out_specs=pl.BlockSpec(memory_space=pl.ANY),
        out_shape=jax.ShapeDtypeStruct((E * n, D), f32),
        scratch_shapes=[pltpu.VMEM((2, tm, D), f32), pltpu.VMEM((2, tm, D), f32), pltpu.SemaphoreType.DMA((2,)),
                        pltpu.SemaphoreType.DMA((2,))],
        compiler_params=_params("arbitrary", "arbitrary"),
        name="moe_ffn",
    )(idx3, idx3, dst.reshape(E * nc, 1, tm), x2, w_gate, w_up, w_down)


def _combine_kernel(kmax_ref, y_hbm, atok_ref, tab_ref, x_ref, g_ref, b_ref, o_ref, ybuf, sem):
    i = pl.program_id(0)
    tt = x_ref.shape[0]
    km = kmax_ref[i]
    aff = atok_ref[...]
    bits = pltpu.bitcast(aff, i32)
    lane = lax.broadcasted_iota(i32, aff.shape, 1)
    tok = i * tt + lax.broadcasted_iota(i32, aff.shape, 0)
    thr, cut = tab_ref[0:1, :], tab_ref[1:2, :]
    sel = (lane < N_EXPERTS) & ((bits > thr) | ((bits == thr) & (tok <= cut)))
    before = jnp.where(lax.broadcasted_iota(i32, (LANES, LANES), 0) < lax.broadcasted_iota(i32, (LANES, LANES), 1), 1.0, 0.0)
    rank = jnp.dot(jnp.where(sel, 1.0, 0.0).astype(EXACT), before.astype(EXACT), preferred_element_type=f32)

    def fetch(k, slot):
        return pltpu.make_async_copy(y_hbm.at[k, pl.ds(i * tt, tt), :], ybuf.at[slot], sem.at[slot])

    @pl.when(km > 0)
    def _():
        fetch(0, 0).start()

    def slot_step(k, h):
        slot = k % 2

        @pl.when(k + 1 < km)
        def _():
            fetch(k + 1, 1 - slot).start()

        fetch(k, slot).wait()
        mine = sel & (rank == lax.convert_element_type(k, f32))
        gk = jnp.sum(jnp.where(mine, aff, 0.0), axis=1, keepdims=True)
        has = jnp.sum(jnp.where(mine, 1.0, 0.0), axis=1, keepdims=True) > 0.0
        return h + jnp.where(has, ybuf[slot], 0.0) * gk

    h = lax.fori_loop(0, km, slot_step, jnp.zeros(x_ref.shape, f32))
    o_ref[...] = _layer_norm_rows(DEEPNORM_ALPHA * x_ref[...] + h, g_ref[...], b_ref[...])


def moe_combine(y, atok, tab, kmax, x2, g, b, tt=LANES):
    n, D = x2.shape
    assert tt == LANES
    return pl.pallas_call(
        _combine_kernel,
        grid_spec=pltpu.PrefetchScalarGridSpec(
            num_scalar_prefetch=1,
            grid=(n // tt,),
            in_specs=[
                pl.BlockSpec(memory_space=pl.ANY),
                pl.BlockSpec((tt, LANES), lambda i, km: (i, 0)),
                pl.BlockSpec((8, LANES), lambda i, km: (0, 0)),
                pl.BlockSpec((tt, D), lambda i, km: (i, 0)),
                pl.BlockSpec((1, D), lambda i, km: (0, 0)),
                pl.BlockSpec((1, D), lambda i, km: (0, 0)),
            ],
            out_specs=pl.BlockSpec((tt, D), lambda i, km: (i, 0)),
            scratch_shapes=[pltpu.VMEM((2, tt, D), f32), pltpu.SemaphoreType.DMA((2,))],
        ),
        out_shape=jax.ShapeDtypeStruct((n, D), f32),
        compiler_params=_params("arbitrary"),
        name="moe_combine",
    )(kmax, y.reshape(N_EXPERTS, n, D), atok, tab, x2, g.reshape(1, D), b.reshape(1, D))


def moe_layer(x, w_router, w_gate, w_up, w_down, ln_g, ln_b):
    B, T, D = x.shape
    n = B * T
    x2 = x.reshape(n, D)
    cap = CAP_FACTOR * n // N_EXPERTS
    atok, a2 = router(x2, w_router)
    idx, dst, tab, kmax = select(a2, cap)
    y = moe_ffn(x2, idx, dst, w_gate, w_up, w_down)
    return moe_combine(y, atok, tab, kmax[:, 0], x2, ln_g, ln_b).reshape(B, T, D)


def _trunk(x, p):
    for i in range(DEPTH):
        j = i // 2
        if i % 2 == 0:
            x = attn_layer(x, p["attn_w_qkv"][j], p["attn_w_o"][j], p["attn_sink"][j], p["ln_mix_g"][i], p["ln_mix_b"][i])
        else:
            x = dn_layer(x, p["dn_w_conv"][j], p["dn_w_z"][j], p["dn_w_ab"][j], p["dn_conv_w"][j], p["dn_a_log"][j],
                         p["dn_dt_bias"][j], p["dn_norm_w"][j], p["dn_w_out"][j], p["ln_mix_g"][i], p["ln_mix_b"][i])
        x = moe_layer(x, p["moe_w_router"][i], p["moe_w_gate"][i], p["moe_w_up"][i], p["moe_w_down"][i],
                      p["ln_ffn_g"][i], p["ln_ffn_b"][i])
    return x


def kernel(x_prompt, x_sample, attn_w_qkv, attn_w_o, attn_sink, dn_w_in, dn_conv_w, dn_a_log, dn_dt_bias, dn_norm_w,
           dn_w_out, moe_w_router, moe_w_gate, moe_w_up, moe_w_down, ln_mix_g, ln_mix_b, ln_ffn_g, ln_ffn_b):
    c = lambda w: w.astype(MXU)
    p = dict(
        attn_w_qkv=c(attn_w_qkv), attn_w_o=c(attn_w_o), attn_sink=attn_sink,
        dn_w_conv=c(dn_w_in[:, :, :CONV_DIM]), dn_w_z=c(dn_w_in[:, :, CONV_DIM:CONV_DIM + V_DIM]),
        dn_w_ab=c(dn_w_in[:, :, CONV_DIM + V_DIM:]), dn_conv_w=dn_conv_w, dn_a_log=dn_a_log, dn_dt_bias=dn_dt_bias,
        dn_norm_w=dn_norm_w, dn_w_out=c(dn_w_out), moe_w_router=moe_w_router, moe_w_gate=c(moe_w_gate),
        moe_w_up=c(moe_w_up), moe_w_down=c(moe_w_down), ln_mix_g=ln_mix_g, ln_mix_b=ln_mix_b, ln_ffn_g=ln_ffn_g,
        ln_ffn_b=ln_ffn_b)
    return _trunk(x_prompt, p), _trunk(x_sample, p)
```

```python
import functools
import math

import jax
import jax.numpy as jnp
from jax import lax
from jax.experimental import pallas as pl
from jax.experimental.pallas import tpu as pltpu

f32, i32 = jnp.float32, jnp.int32
MXU = jnp.bfloat16

D_MODEL = 2048
DEPTH = 4
HQ, HKV, HD = 16, 4, 128
GROUP = HQ // HKV
BLOCK = 128
ROT_DIM = HD // 4
ROPE_THETA = 500000.0
HK, HV, DK, DV = 16, 32, 128, 128
QK_DIM, V_DIM = HK * DK, HV * DV
CONV_DIM = 2 * QK_DIM + V_DIM
CONV_K = 5
CHUNK = 64
N_EXPERTS = 16
D_EXPERT = 1024
CAP_FACTOR = 2
DEEPNORM_ALPHA = (2.0 * DEPTH) ** 0.25
LN_EPS = 1e-5
RMS_EPS = 1e-6

LANES = 128
V7X_VMEM_LIMIT = 56 * 1024 * 1024


def _params(*sem):
    return pltpu.CompilerParams(dimension_semantics=sem, vmem_limit_bytes=V7X_VMEM_LIMIT)


def _mm_kernel(a_ref, w_ref, o_ref):
    o_ref[...] = jnp.dot(a_ref[...].astype(MXU), w_ref[...], preferred_element_type=f32).astype(o_ref.dtype)


def matmul(a, w, out_dtype=f32, tm=1024, tn=1024):
    M, K = a.shape
    N = w.shape[1]
    tm, tn = min(tm, M), min(tn, N)
    assert M % tm == 0 and N % tn == 0
    return pl.pallas_call(
        _mm_kernel,
        grid=(N // tn, M // tm),
        in_specs=[pl.BlockSpec((tm, K), lambda j, i: (i, 0)), pl.BlockSpec((K, tn), lambda j, i: (0, j))],
        out_specs=pl.BlockSpec((tm, tn), lambda j, i: (i, j)),
        out_shape=jax.ShapeDtypeStruct((M, N), out_dtype),
        compiler_params=_params("parallel", "parallel"),
        name="matmul",
    )(a, w)


def _layer_norm_rows(y, g, b):
    mu = jnp.mean(y, axis=-1, keepdims=True)
    yc = y - mu
    var = jnp.mean(yc * yc, axis=-1, keepdims=True)
    return yc * lax.rsqrt(var + LN_EPS) * g + b


def _mm_ln_kernel(a_ref, w_ref, x_ref, g_ref, b_ref, o_ref):
    h = jnp.dot(a_ref[...].astype(MXU), w_ref[...], preferred_element_type=f32)
    o_ref[...] = _layer_norm_rows(DEEPNORM_ALPHA * x_ref[...] + h, g_ref[...], b_ref[...])


def matmul_res_ln(a, w, x, g, b, tm=256):
    M, K = a.shape
    D = w.shape[1]
    tm = min(tm, M)
    assert M % tm == 0
    return pl.pallas_call(
        _mm_ln_kernel,
        grid=(M // tm,),
        in_specs=[
            pl.BlockSpec((tm, K), lambda i: (i, 0)),
            pl.BlockSpec((K, D), lambda i: (0, 0), pipeline_mode=pl.Buffered(1)),
            pl.BlockSpec((tm, D), lambda i: (i, 0)),
            pl.BlockSpec((1, D), lambda i: (0, 0)),
            pl.BlockSpec((1, D), lambda i: (0, 0)),
        ],
        out_specs=pl.BlockSpec((tm, D), lambda i: (i, 0)),
        out_shape=jax.ShapeDtypeStruct((M, D), f32),
        compiler_params=_params("parallel"),
        name="matmul_res_ln",
    )(a, w, x, g.reshape(1, D), b.reshape(1, D))


def rope_table(T):
    half = ROT_DIM // 2
    inv = ROPE_THETA ** (-jnp.arange(0, ROT_DIM, 2, dtype=f32) / ROT_DIM)
    ang = jnp.arange(T, dtype=f32)[:, None] * inv[None, :]
    c, s = jnp.cos(ang), jnp.sin(ang)
    z = jnp.zeros((T, HD - ROT_DIM), f32)
    zh = jnp.zeros((T, half), f32)
    return jnp.concatenate([c, c, z + 1.0, -s, zh, z, zh, s, z], axis=1)


def _rope(t, tab):
    return t * tab[:, :HD] + pltpu.roll(t, HD - ROT_DIM // 2, 1) * tab[:, HD:2 * HD] + pltpu.roll(t, ROT_DIM // 2, 1) * tab[:, 2 * HD:]


def _attn_kernel(sink_ref, q_ref, kp_ref, kc_ref, kn_ref, vp_ref, vc_ref, vn_ref, tp_ref, tc_ref, tn_ref, o_ref):
    n = pl.program_id(1)
    nb = pl.num_programs(1)
    ii = lax.broadcasted_iota(i32, (GROUP * BLOCK, 3 * BLOCK), 0) % BLOCK
    jj = lax.broadcasted_iota(i32, (GROUP * BLOCK, 3 * BLOCK), 1)
    lo = jnp.where(n == 0, BLOCK, 0)
    hi = jnp.where(n == nb - 1, 2 * BLOCK, 3 * BLOCK)
    valid = (jj >= ii) & (jj <= ii + 2 * BLOCK) & (jj >= lo) & (jj < hi)
    tq = tc_ref[...]
    tabs = (tp_ref[...], tq, tn_ref[...])
    for h in range(HKV):
        hs = slice(h * HD, (h + 1) * HD)
        kb = jnp.concatenate([_rope(r[0, :, hs], t) for r, t in zip((kp_ref, kc_ref, kn_ref), tabs)], axis=0).astype(MXU)
        vb = jnp.concatenate([r[0, :, hs] for r in (vp_ref, vc_ref, vn_ref)], axis=0).astype(MXU)
        qh = jnp.concatenate([_rope(q_ref[0, :, (h * GROUP + g) * HD:(h * GROUP + g + 1) * HD], tq) for g in range(GROUP)],
                             axis=0).astype(MXU)
        s = lax.dot_general(qh, kb, (((1,), (1,)), ((), ())), preferred_element_type=f32) * (HD ** -0.5)
        s = jnp.where(valid, s, -jnp.inf)
        sk = jnp.concatenate([jnp.full((BLOCK, 1), sink_ref[0, h * GROUP + g], f32) for g in range(GROUP)], axis=0)
        m = jnp.maximum(jnp.max(s, axis=-1, keepdims=True), sk)
        p = jnp.exp(s - m)
        denom = jnp.sum(p, axis=-1, keepdims=True) + jnp.exp(sk - m)
        p = (p / denom).astype(MXU)
        o = jnp.dot(p, vb, preferred_element_type=f32)
        for g in range(GROUP):
            o_ref[0, :, (h * GROUP + g) * HD:(h * GROUP + g + 1) * HD] = o[g * BLOCK:(g + 1) * BLOCK].astype(o_ref.dtype)


def attention(qkv, sink):
    B, T, _ = qkv.shape
    out_dtype = MXU
    nb = T // BLOCK
    tab = rope_table(T)
    kcol, vcol = HQ // HKV, HQ // HKV + 1
    kvw = HKV * HD
    prev = lambda n: jnp.maximum(n - 1, 0)
    nxt = lambda n: jnp.minimum(n + 1, nb - 1)
    return pl.pallas_call(
        _attn_kernel,
        grid=(B, nb),
        in_specs=[
            pl.BlockSpec(memory_space=pltpu.SMEM),
            pl.BlockSpec((1, BLOCK, HQ * HD), lambda b, n: (b, n, 0)),
            pl.BlockSpec((1, BLOCK, kvw), lambda b, n: (b, prev(n), kcol)),
            pl.BlockSpec((1, BLOCK, kvw), lambda b, n: (b, n, kcol)),
            pl.BlockSpec((1, BLOCK, kvw), lambda b, n: (b, nxt(n), kcol)),
            pl.BlockSpec((1, BLOCK, kvw), lambda b, n: (b, prev(n), vcol)),
            pl.BlockSpec((1, BLOCK, kvw), lambda b, n: (b, n, vcol)),
            pl.BlockSpec((1, BLOCK, kvw), lambda b, n: (b, nxt(n), vcol)),
            pl.BlockSpec((BLOCK, 3 * HD), lambda b, n: (prev(n), 0)),
            pl.BlockSpec((BLOCK, 3 * HD), lambda b, n: (n, 0)),
            pl.BlockSpec((BLOCK, 3 * HD), lambda b, n: (nxt(n), 0)),
        ],
        out_specs=pl.BlockSpec((1, BLOCK, HQ * HD), lambda b, n: (b, n, 0)),
        out_shape=jax.ShapeDtypeStruct((B, T, HQ * HD), out_dtype),
        compiler_params=_params("parallel", "parallel"),
        name="attention",
    )(sink.reshape(1, HQ).astype(f32), qkv, qkv, qkv, qkv, qkv, qkv, qkv, tab, tab, tab)


def attn_layer(x, w_qkv, w_o, sink, ln_g, ln_b):
    B, T, D = x.shape
    x2 = x.reshape(B * T, D)
    qkv = matmul(x2, w_qkv).reshape(B, T, -1)
    o = attention(qkv, sink).reshape(B * T, HQ * HD)
    return matmul_res_ln(o, w_o, x2, ln_g, ln_b).reshape(B, T, D)


def _silu(x):
    return x * jax.nn.sigmoid(x)


def _conv_kernel(xp_ref, xc_ref, xn_ref, w_ref, o_ref, *, l2, scale):
    i = pl.program_id(1)
    tt = xc_ref.shape[1]
    pad = CONV_K // 2
    xp = jnp.where(i == 0, 0.0, xp_ref[0])
    xn = jnp.where(i == pl.num_programs(1) - 1, 0.0, xn_ref[0])
    xe = jnp.concatenate([xp, xc_ref[0], xn], axis=0)
    acc = xe[8 - pad:8 - pad + tt] * w_ref[0:1, :]
    for j in range(1, CONV_K):
        acc = acc + xe[8 - pad + j:8 - pad + j + tt] * w_ref[j:j + 1, :]
    y = _silu(acc)
    for h in range(y.shape[1] // DK):
        seg = y[:, h * DK:(h + 1) * DK]
        if l2:
            seg = seg * lax.rsqrt(jnp.sum(seg * seg, axis=-1, keepdims=True) + RMS_EPS) * scale
        o_ref[0, :, h * DK:(h + 1) * DK] = seg


def _conv_part(proj, conv_w, col0, ncols, l2, scale, tt=512, cb=512):
    B, T, _ = proj.shape
    tt = min(tt, T)
    assert T % tt == 0 and col0 % cb == 0 and ncols % cb == 0
    c0 = col0 // cb
    r8 = tt // 8
    return pl.pallas_call(
        functools.partial(_conv_kernel, l2=l2, scale=scale),
        grid=(B, T // tt, ncols // cb),
        in_specs=[
            pl.BlockSpec((1, 8, cb), lambda b, i, c: (b, jnp.maximum(i * r8 - 1, 0), c0 + c)),
            pl.BlockSpec((1, tt, cb), lambda b, i, c: (b, i, c0 + c)),
            pl.BlockSpec((1, 8, cb), lambda b, i, c: (b, jnp.minimum((i + 1) * r8, T // 8 - 1), c0 + c)),
            pl.BlockSpec((CONV_K, cb), lambda b, i, c: (0, c0 + c)),
        ],
        out_specs=pl.BlockSpec((1, tt, cb), lambda b, i, c: (b, i, c)),
        out_shape=jax.ShapeDtypeStruct((B, T, ncols), f32),
        compiler_params=_params("parallel", "parallel", "parallel"),
        name="dn_conv",
    )(proj, proj, proj, conv_w)


def conv_qkv(proj, conv_w):
    q = _conv_part(proj, conv_w, 0, QK_DIM, True, DK ** -0.5)
    k = _conv_part(proj, conv_w, QK_DIM, QK_DIM, True, 1.0)
    v = _conv_part(proj, conv_w, 2 * QK_DIM, V_DIM, False, 1.0)
    return q, k, v


def _gates_kernel(ab_ref, alog_ref, dtb_ref, o_ref):
    x = ab_ref[0]
    tt = x.shape[0]
    g = -jnp.exp(alog_ref[...]) * jax.nn.softplus(x + dtb_ref[...])
    beta = jax.nn.sigmoid(x)
    ri = lax.broadcasted_iota(i32, (tt, tt), 0)
    ci = lax.broadcasted_iota(i32, (tt, tt), 1)
    same = (ri // CHUNK) == (ci // CHUNK)
    lp = jnp.where(same & (ci <= ri), 1.0, 0.0).astype(f32)
    ls = jnp.where(same & (ci >= ri), 1.0, 0.0).astype(f32)
    gp = jnp.dot(lp, g, precision=lax.Precision.HIGHEST, preferred_element_type=f32)
    gs = jnp.dot(ls, g, precision=lax.Precision.HIGHEST, preferred_element_type=f32)
    lane = lax.broadcasted_iota(i32, x.shape, 1)
    cum = jnp.where(lane < 2 * HV, gp, gs)
    o_ref[0] = jnp.where((lane % (2 * HV)) < HV, cum, beta)


def gates(ab, a_log, dt_bias, tt=256):
    B, T, W = ab.shape
    tt = min(tt, T)
    z = jnp.zeros((2, HV), f32)
    alog = jnp.concatenate([a_log.astype(f32), z], axis=1).reshape(1, W)
    dtb = jnp.concatenate([dt_bias.astype(f32), z], axis=1).reshape(1, W)
    return pl.pallas_call(
        _gates_kernel,
        grid=(B, T // tt),
        in_specs=[pl.BlockSpec((1, tt, W), lambda b, i: (b, i, 0)), pl.BlockSpec((1, W), lambda b, i: (0, 0)),
                  pl.BlockSpec((1, W), lambda b, i: (0, 0))],
        out_specs=pl.BlockSpec((1, tt, W), lambda b, i: (b, i, 0)),
        out_shape=jax.ShapeDtypeStruct((B, T, W), f32),
        compiler_params=_params("parallel", "parallel"),
        name="dn_gates",
    )(ab, alog, dtb)


def _bmm(a, b):
    return jnp.einsum("bik,bkj->bij", a.astype(MXU), b.astype(MXU), preferred_element_type=f32)


def _unit_tri_inverse(L):
    c = L.shape[-1]
    eye = (lax.broadcasted_iota(i32, (c, c), 0) == lax.broadcasted_iota(i32, (c, c), 1)).astype(f32)
    m = -L
    p = eye[None] + m
    m = _bmm(m, m)
    for _ in range(int(math.log2(c)) - 2):
        r = _bmm(jnp.concatenate([m, p], axis=1), m)
        m, p = r[:, :c], p + r[:, c:]
    return p + _bmm(p, m)


def _delta_kernel(qf, kf, vf, gf, qb, kb, vb, gb, of, ob, s_ref, *, hb):
    C = CHUNK
    rep = HV // HK
    h0 = pl.program_id(1) * hb

    @pl.when(pl.program_id(2) == 0)
    def _():
        s_ref[...] = jnp.zeros_like(s_ref)

    tb = qf.shape[1]
    nc = tb // C
    ri = lax.broadcasted_iota(i32, (C, C), 0)
    ci = lax.broadcasted_iota(i32, (C, C), 1)
    eye = (ri == ci)[None]
    Ls, rhs, lhs2, qeg, egl = [], [], [], [], []
    for reverse, (q_ref, k_ref, v_ref, g_ref) in enumerate(((qf, kf, vf, gf), (qb, kb, vb, gb))):
        incl = ((ri <= ci) if reverse else (ri >= ci))[None]
        strict = ((ri < ci) if reverse else (ri > ci))[None]
        gt = g_ref[0]
        lane = lax.broadcasted_iota(i32, gt.shape, 1)
        for h in range(hb):
            k3 = k_ref[0, :, h * DK:(h + 1) * DK].reshape(nc, C, DK)
            q3 = q_ref[0, :, h * DK:(h + 1) * DK].reshape(nc, C, DK)
            gram = jnp.einsum("cid,cjd->cij", jnp.concatenate([k3, q3], axis=1).astype(MXU), k3.astype(MXU),
                              preferred_element_type=f32)
            kk, qk = gram[:, :C], gram[:, C:]
            kT3 = jnp.concatenate([k3[c].T[None] for c in range(nc)], axis=0)
            for j in range(rep):
                lane_g = (2 * HV if reverse else 0) + rep * (h0 + h) + j
                gcol = jnp.sum(jnp.where(lane == lane_g, gt, 0.0), axis=1, keepdims=True).reshape(nc, C, 1)
                bcol = jnp.sum(jnp.where(lane == lane_g + HV, gt, 0.0), axis=1, keepdims=True).reshape(nc, C, 1)
                grow = jnp.sum(jnp.where(eye, gcol, 0.0), axis=1, keepdims=True)
                glast = gcol[:, 0:1, :] if reverse else gcol[:, C - 1:C, :]
                decay = jnp.exp(jnp.where(incl, gcol - grow, -jnp.inf))
                eg = jnp.exp(gcol)
                v3 = v_ref[0, :, (h * rep + j) * DV:(h * rep + j + 1) * DV].reshape(nc, C, DV)
                Ls.append(jnp.where(strict, bcol * kk * decay, 0.0))
                rhs.append(jnp.concatenate([v3 * bcol, k3 * (bcol * eg)], axis=2))
                lhs2.append(jnp.concatenate([kT3 * jnp.exp(glast - grow), qk * decay], axis=1))
                qeg.append(q3 * eg)
                egl.append(jnp.exp(glast))
    cat = lambda xs: jnp.concatenate(xs, axis=0)
    sol = _bmm(_unit_tri_inverse(cat(Ls)), cat(rhs))
    r2 = _bmm(cat(lhs2), sol)
    aq = jnp.concatenate([r2[:, :DK, DV:], cat(qeg) - r2[:, DK:, DV:]], axis=1).astype(MXU)
    bm, op, egl = r2[:, :DK, :DV], r2[:, DK:, :DV], cat(egl)
    nch = 2 * hb * rep
    S = [s_ref[ch] for ch in range(nch)]
    for step in range(nc):
        for ch in range(nch):
            reverse, col = divmod(ch, hb * rep)
            c = nc - 1 - step if reverse else step
            i = ch * nc + c
            r = jnp.dot(aq[i], S[ch].astype(MXU), preferred_element_type=f32)
            (ob if reverse else of)[0, c * C:(c + 1) * C, col * DV:(col + 1) * DV] = r[DK:] + op[i]
            S[ch] = S[ch] * egl[i] - r[:DK] + bm[i]
    for ch in range(nch):
        s_ref[ch] = S[ch]


def delta_rule(q, k, v, gt, tb=256, hb=2):
    B, T, _ = q.shape
    tb = min(tb, T)
    nT = T // tb
    rep = HV // HK
    fwd = lambda b, h, i: (b, i, h)
    bwd = lambda b, h, i: (b, nT - 1 - i, h)
    fwd0 = lambda b, h, i: (b, i, 0)
    bwd0 = lambda b, h, i: (b, nT - 1 - i, 0)
    qk_spec = lambda m: pl.BlockSpec((1, tb, hb * DK), m)
    v_spec = lambda m: pl.BlockSpec((1, tb, hb * rep * DV), m)
    g_spec = lambda m: pl.BlockSpec((1, tb, gt.shape[2]), m)
    o_shape = jax.ShapeDtypeStruct((B, T, V_DIM), f32)
    return pl.pallas_call(
        functools.partial(_delta_kernel, hb=hb),
        grid=(B, HK // hb, nT),
        in_specs=[qk_spec(fwd), qk_spec(fwd), v_spec(fwd), g_spec(fwd0), qk_spec(bwd), qk_spec(bwd), v_spec(bwd), g_spec(bwd0)],
        out_specs=[v_spec(fwd), v_spec(bwd)],
        out_shape=[o_shape, o_shape],
        scratch_shapes=[pltpu.VMEM((2 * hb * rep, DK, DV), f32)],
        compiler_params=_params("parallel", "parallel", "arbitrary"),
        name="dn_delta",
    )(q, k, v, gt, q, k, v, gt)


def _gnorm_kernel(of_ref, ob_ref, z_ref, nw_ref, o_ref):
    o = of_ref[...] + ob_ref[...]
    z = z_ref[...]
    for h in range(o.shape[1] // DV):
        hs = slice(h * DV, (h + 1) * DV)
        seg = o[:, hs]
        y = seg * lax.rsqrt(jnp.mean(seg * seg, axis=-1, keepdims=True) + RMS_EPS) * nw_ref[...]
        o_ref[:, hs] = (y * _silu(z[:, hs])).astype(o_ref.dtype)


def gated_norm(of, ob, z, norm_w, tt=256, cb=1024):
    M, W = of.shape
    tt = min(tt, M)
    spec = pl.BlockSpec((tt, cb), lambda i, c: (i, c))
    return pl.pallas_call(
        _gnorm_kernel,
        grid=(M // tt, W // cb),
        in_specs=[spec, spec, spec, pl.BlockSpec((1, DV), lambda i, c: (0, 0))],
        out_specs=spec,
        out_shape=jax.ShapeDtypeStruct((M, W), MXU),
        compiler_params=_params("parallel", "parallel"),
        name="dn_gated_norm",
    )(of, ob, z, norm_w.reshape(1, DV).astype(f32))


def dn_layer(x, w_conv, w_z, w_ab, conv_w, a_log, dt_bias, norm_w, w_out, ln_g, ln_b):
    B, T, D = x.shape
    x2 = x.reshape(B * T, D)
    pc = matmul(x2, w_conv).reshape(B, T, CONV_DIM)
    z = matmul(x2, w_z)
    ab = matmul(x2, w_ab).reshape(B, T, -1)
    q, k, v = conv_qkv(pc, conv_w)
    gt = gates(ab, a_log, dt_bias)
    of, ob = delta_rule(q, k, v, gt)
    o = gated_norm(of.reshape(B * T, V_DIM), ob.reshape(B * T, V_DIM), z, norm_w)
    return matmul_res_ln(o, w_out, x2, ln_g, ln_b).reshape(B, T, D)


EXACT = jnp.bfloat16


def _split3(x):
    hi = x.astype(MXU)
    return hi, (x - hi.astype(f32)).astype(MXU)


def _router_kernel(x_ref, w_ref, atok_ref, a2_ref):
    xh, xl = _split3(x_ref[...])
    wh, wl = _split3(w_ref[...])
    dot = functools.partial(jnp.dot, preferred_element_type=f32)
    logits = dot(xh, wh) + (dot(xl, wh) + dot(xh, wl))
    lane = lax.broadcasted_iota(i32, logits.shape, 1)
    lm = jnp.where(lane < N_EXPERTS, logits, -jnp.inf)
    e = jnp.exp(lm - jnp.max(lm, axis=-1, keepdims=True))
    aff = e / jnp.sum(e, axis=-1, keepdims=True)
    atok_ref[...] = aff
    for j in range(aff.shape[0] // LANES):
        bt = aff[j * LANES:(j + 1) * LANES, :].T
        for ex in range(N_EXPERTS):
            a2_ref[j:j + 1, ex * LANES:(ex + 1) * LANES] = bt[ex:ex + 1, :]


def router(x2, w_router, tm=1024):
    n, D = x2.shape
    tm = min(tm, n)
    w = jnp.zeros((D, LANES), f32).at[:, :N_EXPERTS].set(w_router.astype(f32))
    return pl.pallas_call(
        _router_kernel,
        grid=(n // tm,),
        in_specs=[pl.BlockSpec((tm, D), lambda i: (i, 0)), pl.BlockSpec((D, LANES), lambda i: (0, 0))],
        out_specs=[pl.BlockSpec((tm, LANES), lambda i: (i, 0)), pl.BlockSpec((tm // LANES, N_EXPERTS * LANES), lambda i: (i, 0))],
        out_shape=[jax.ShapeDtypeStruct((n, LANES), f32), jax.ShapeDtypeStruct((n // LANES, N_EXPERTS * LANES), f32)],
        compiler_params=_params("parallel"),
        name="moe_router",
    )(x2, w)


def _per_expert(row, op):
    parts = []
    for ex in range(N_EXPERTS):
        s = op(row[:, ex * LANES:(ex + 1) * LANES], axis=1, keepdims=True)
        parts.append(jnp.broadcast_to(s, (1, LANES)))
    return jnp.concatenate(parts, axis=1)


def _select_kernel(a_ref, idx_ref, dst_ref, tab_ref, kmax_ref, *, cap, n):
    R = a_ref.shape[0]
    W = N_EXPERTS * LANES
    bits = pltpu.bitcast(a_ref[...], i32)
    capf = jnp.float32(cap)

    def count(mask):
        return _per_expert(jnp.sum(jnp.where(mask, 1.0, 0.0), axis=0, keepdims=True), jnp.sum)

    def bit_step(t, ans):
        cand = ans | lax.shift_left(jnp.int32(1), 29 - t)
        return jnp.where(count(bits >= cand) >= capf, cand, ans)

    thr = lax.fori_loop(0, 30, bit_step, jnp.zeros((1, W), i32))
    gt = bits > thr
    eq = bits == thr
    need = capf - count(gt)

    lane_r = lax.broadcasted_iota(i32, (LANES, LANES), 0)
    lane_c = lax.broadcasted_iota(i32, (LANES, LANES), 1)
    upper = jnp.where(lane_r <= lane_c, 1.0, 0.0).astype(EXACT)
    row_r = lax.broadcasted_iota(i32, (R, R), 0)
    row_c = lax.broadcasted_iota(i32, (R, R), 1)
    below = jnp.where(row_c < row_r, 1.0, 0.0).astype(EXACT)

    def prefix(slab):
        lc = jnp.dot(slab.astype(EXACT), upper, preferred_element_type=f32)
        tot = jnp.broadcast_to(lc[:, LANES - 1:LANES], (R, LANES))
        off = jnp.dot(below, tot.astype(EXACT), preferred_element_type=f32)
        return lc, off, tot

    tok = (lax.broadcasted_iota(i32, (R, LANES), 0) * LANES + lax.broadcasted_iota(i32, (R, LANES), 1))
    sub_r = lax.broadcasted_iota(i32, (R, LANES), 0).astype(f32)
    sub_l = lax.broadcasted_iota(i32, (LANES, LANES), 0).astype(f32)
    lane_p = lax.broadcasted_iota(i32, (1, LANES), 1)
    nsel = jnp.zeros((R, LANES), f32)
    cut_parts = []
    for ex in range(N_EXPERTS):
        es = slice(ex * LANES, (ex + 1) * LANES)
        eq_e = jnp.where(eq[:, es], 1.0, 0.0)
        lc, off, _ = prefix(eq_e)
        take = eq[:, es] & (lc + off <= need[:, es])
        cut_parts.append(jnp.broadcast_to(jnp.max(jnp.max(jnp.where(take, tok, -1), axis=0, keepdims=True), axis=1, keepdims=True), (1, LANES)))
        sel_e = jnp.where(gt[:, es] | take, 1.0, 0.0)
        lc, off, tot = prefix(sel_e)
        rank_e = nsel
        nsel = nsel + sel_e
        hi = jnp.floor(off * (1.0 / LANES))
        lo = off - hi * LANES
        table = jnp.concatenate([lc.T, rank_e.T, hi.T[0:8], lo.T[0:8]], axis=0).astype(EXACT)
        end = off + tot

        def chunk(c, carry):
            c = lax.convert_element_type(c, i32)
            p =(c * LANES + lane_p).astype(f32)
            jp = jnp.sum(jnp.where(end <= p, 1.0, 0.0), axis=0, keepdims=True)
            onehot = jnp.where(sub_r == jp, 1.0, 0.0).astype(EXACT)
            g = jnp.dot(table, onehot, preferred_element_type=f32)
            p_loc = p - (g[2 * LANES:2 * LANES + 1] * LANES + g[2 * LANES + 8:2 * LANES + 9])
            lp = jnp.sum(jnp.where(g[:LANES] <= p_loc, 1.0, 0.0), axis=0, keepdims=True)
            rk = jnp.sum(jnp.where(sub_l == lp, g[LANES:2 * LANES], 0.0), axis=0, keepdims=True)
            t = jp * LANES + lp
            col = pl.ds(pl.multiple_of(c * LANES, LANES), LANES)
            idx_ref[ex:ex + 1, col] = t.astype(i32)
            dst_ref[ex:ex + 1, col] = (rk * n + t).astype(i32)
            return carry

        lax.fori_loop(0, cap // LANES, chunk, 0)

    diag = lax.broadcasted_iota(i32, (N_EXPERTS, LANES), 0) == lax.broadcasted_iota(i32, (N_EXPERTS, LANES), 1)
    thr_rows = jnp.concatenate([thr[:, ex * LANES:(ex + 1) * LANES] for ex in range(N_EXPERTS)], axis=0)
    cut_rows = jnp.concatenate(cut_parts, axis=0)
    tab_ref[...] = jnp.zeros_like(tab_ref)
    tab_ref[0:1, :] = jnp.sum(jnp.where(diag, thr_rows, 0), axis=0, keepdims=True)
    tab_ref[1:2, :] = jnp.sum(jnp.where(diag, cut_rows, 0), axis=0, keepdims=True)
    kmax_ref[...] = jnp.broadcast_to(jnp.max(nsel, axis=1, keepdims=True), (R, LANES)).astype(i32)


def select(a2, cap):
    R = a2.shape[0]
    n = R * LANES
    assert cap % LANES == 0
    return pl.pallas_call(
        functools.partial(_select_kernel, cap=cap, n=n),
        out_shape=[jax.ShapeDtypeStruct((N_EXPERTS, cap), i32), jax.ShapeDtypeStruct((N_EXPERTS, cap), i32),
                   jax.ShapeDtypeStruct((8, LANES), i32), jax.ShapeDtypeStruct((R, LANES), i32)],
        compiler_params=pltpu.CompilerParams(vmem_limit_bytes=V7X_VMEM_LIMIT),
        name="moe_select",
    )(a2)


ROW_DMA_UNROLL = 8

def _ffn_kernel(idx_ref, nidx_ref, dst_ref, x_hbm, wg_ref, wu_ref, wd_ref, y_hbm, xbuf, ybuf, gsem, ssem):
    tm = xbuf.shape[1]
    s = pl.program_id(0) * pl.num_programs(1) + pl.program_id(1)
    last = pl.num_programs(0) * pl.num_programs(1) - 1
    slot = s % 2

    def gather(ref, r, sl):
        return pltpu.make_async_copy(x_hbm.at[pl.ds(ref[0, 0, r], 1), :], xbuf.at[sl, pl.ds(r, 1), :], gsem.at[sl])

    def scatter(r, sl):
        return pltpu.make_async_copy(ybuf.at[sl, pl.ds(r, 1), :], y_hbm.at[pl.ds(dst_ref[0, 0, r], 1), :], ssem.at[sl])

    def each_row(fn):
        def body(r, c):
            fn(r)
            return c
        lax.fori_loop(0, tm, body, 0, unroll=ROW_DMA_UNROLL)

    @pl.when(s == 0)
    def _():
        each_row(lambda r: gather(idx_ref, r, 0).start())

    each_row(lambda r: gather(idx_ref, r, slot).wait())

    @pl.when(s >= 2)
    def _():
        each_row(lambda r: scatter(r, slot).wait())

    for r in range(tm):
        gather(nidx_ref, r, 1 - slot).start()
    xe = xbuf[slot].astype(MXU)
    gate = jnp.dot(xe, wg_ref[...], preferred_element_type=f32)
    up = jnp.dot(xe, wu_ref[...], preferred_element_type=f32)
    h = (_silu(gate) * up).astype(MXU)
    ybuf[slot] = jnp.dot(h, wd_ref[...], preferred_element_type=f32)
    for r in range(tm):
        scatter(r, slot).start()

    @pl.when(s == last)
    def _():
        each_row(lambda r: gather(nidx_ref, r, 1 - slot).wait())

        @pl.when(s >= 1)
        def _():
            each_row(lambda r: scatter(r, 1 - slot).wait())

        each_row(lambda r: scatter(r, slot).wait())


def moe_ffn(x2, idx, dst, w_gate, w_up, w_down, tm=256):
    n, D = x2.shape
    E, cap = idx.shape
    F = w_gate.shape[2]
    tm = min(tm, cap)
    nc = cap // tm
    ispec = pl.BlockSpec((1, 1, tm), lambda e, c: (e * nc + c, 0, 0), memory_space=pltpu.SMEM)
    nspec = pl.BlockSpec((1, 1, tm), lambda e, c: (jnp.minimum(e * nc + c + 1, E * nc - 1), 0, 0), memory_space=pltpu.SMEM)
    idx3 = idx.reshape(E * nc, 1, tm)
    return pl.pallas_call(
        _ffn_kernel,
        grid=(E, nc),
        in_specs=[
            ispec, nspec, ispec,
            pl.BlockSpec(memory_space=pl.ANY),
            pl.BlockSpec((None, D, F), lambda e, c: (e, 0, 0)),
            pl.BlockSpec((None, D, F), lambda e, c: (e, 0, 0)),
            pl.BlockSpec((None, F, D), lambda e, c: (e, 0, 0)),
        ],
        out_specs=pl.BlockSpec(memory_space=pl.ANY),
        out_shape=jax.ShapeDtypeStruct((E * n, D), f32),
        scratch_shapes=[pltpu.VMEM((2, tm, D), f32), pltpu.VMEM((2, tm, D), f32), pltpu.SemaphoreType.DMA((2,)),
                        pltpu.SemaphoreType.DMA((2,))],
        compiler_params=_params("arbitrary", "arbitrary"),
        name="moe_ffn",
    )(idx3, idx3, dst.reshape(E * nc, 1, tm), x2, w_gate, w_up, w_down)


COMBINE_COLS = 256


def _combine_kernel(kmax_ref, y_hbm, atok_ref, tab_ref, x_ref, g_ref, b_ref, o_ref, ybuf, gbuf, hbuf, sem):
    i = pl.program_id(0)
    nt = pl.num_programs(0)
    tt, D = x_ref.shape
    half = i % 2
    km = kmax_ref[i]

    def fetch(tile, k, hf):
        return pltpu.make_async_copy(y_hbm.at[pl.ds(k * (nt * tt) + tile * tt, tt), :], ybuf.at[hf, k], sem.at[hf, k])

    def each_slot(count, fn):
        def body(k, c):
            fn(k)
            return c
        lax.fori_loop(0, count, body, 0)

    @pl.when(i == 0)
    def _():
        each_slot(km, lambda k: fetch(0, k, 0).start())

    @pl.when(i + 1 < nt)
    def _():
        each_slot(kmax_ref[jnp.minimum(i + 1, nt - 1)], lambda k: fetch(i + 1, k, 1 - half).start())

    aff = atok_ref[...]
    bits = pltpu.bitcast(aff, i32)
    lane = lax.broadcasted_iota(i32, aff.shape, 1)
    tok = i * tt + lax.broadcasted_iota(i32, aff.shape, 0)
    thr, cut = tab_ref[0:1, :], tab_ref[1:2, :]
    sel = (lane < N_EXPERTS) & ((bits > thr) | ((bits == thr) & (tok <= cut)))
    before = jnp.where(lax.broadcasted_iota(i32, (LANES, LANES), 0) < lax.broadcasted_iota(i32, (LANES, LANES), 1), 1.0, 0.0)
    rank = jnp.dot(jnp.where(sel, 1.0, 0.0).astype(EXACT), before.astype(EXACT), preferred_element_type=f32)

    def slot_gate(k):
        mine = sel & (rank == lax.convert_element_type(k, f32))
        gk = jnp.sum(jnp.where(mine, aff, 0.0), axis=1, keepdims=True)
        has = jnp.sum(jnp.where(mine, 1.0, 0.0), axis=1, keepdims=True) > 0.0
        gbuf[k] = jnp.broadcast_to(jnp.where(has, gk, -1.0), (tt, LANES))

    each_slot(km, slot_gate)
    each_slot(km, lambda k: fetch(i, k, half).wait())
    reps = COMBINE_COLS // LANES
    for cc in range(D // COMBINE_COLS):
        cs = slice(cc * COMBINE_COLS, (cc + 1) * COMBINE_COLS)

        def add_slot(k, acc):
            gk = jnp.concatenate([gbuf[k]] * reps, axis=1)
            return acc + jnp.where(gk >= 0.0, ybuf[half, k, :, cs], 0.0) * gk

        hbuf[:, cs] = lax.fori_loop(0, km, add_slot, jnp.zeros((tt, COMBINE_COLS), f32))
    o_ref[...] = _layer_norm_rows(DEEPNORM_ALPHA * x_ref[...] + hbuf[...], g_ref[...], b_ref[...])


def moe_combine(y, atok, tab, kmax, x2, g, b, tt=LANES):
    n, D = x2.shape
    assert tt == LANES
    return pl.pallas_call(
        _combine_kernel,
        grid_spec=pltpu.PrefetchScalarGridSpec(
            num_scalar_prefetch=1,
            grid=(n // tt,),
            in_specs=[
                pl.BlockSpec(memory_space=pl.ANY),
                pl.BlockSpec((tt, LANES), lambda i, km: (i, 0)),
                pl.BlockSpec((8, LANES), lambda i, km: (0, 0)),
                pl.BlockSpec((tt, D), lambda i, km: (i, 0)),
                pl.BlockSpec((1, D), lambda i, km: (0, 0)),
                pl.BlockSpec((1, D), lambda i, km: (0, 0)),
            ],
            out_specs=pl.BlockSpec((tt, D), lambda i, km: (i, 0)),
            scratch_shapes=[pltpu.VMEM((2, N_EXPERTS, tt, D), f32), pltpu.VMEM((N_EXPERTS, tt, LANES), f32),
                            pltpu.VMEM((tt, D), f32), pltpu.SemaphoreType.DMA((2, N_EXPERTS))],
        ),
        out_shape=jax.ShapeDtypeStruct((n, D), f32),
        compiler_params=_params("arbitrary"),
        name="moe_combine",
    )(kmax, y, atok, tab, x2, g.reshape(1, D), b.reshape(1, D))


def moe_layer(x, w_router, w_gate, w_up, w_down, ln_g, ln_b):
    B, T, D = x.shape
    n = B * T
    x2 = x.reshape(n, D)
    cap = CAP_FACTOR * n // N_EXPERTS
    atok, a2 = router(x2, w_router)
    idx, dst, tab, kmax = select(a2, cap)
    y = moe_ffn(x2, idx, dst, w_gate, w_up, w_down)
    return moe_combine(y, atok, tab, kmax[:, 0], x2, ln_g, ln_b).reshape(B, T, D)


def _trunk(x, p):
    for i in range(DEPTH):
        j = i // 2
        if i % 2 == 0:
            x = attn_layer(x, p["attn_w_qkv"][j], p["attn_w_o"][j], p["attn_sink"][j], p["ln_mix_g"][i], p["ln_mix_b"][i])
        else:
            x = dn_layer(x, p["dn_w_conv"][j], p["dn_w_z"][j], p["dn_w_ab"][j], p["dn_conv_w"][j], p["dn_a_log"][j],
                         p["dn_dt_bias"][j], p["dn_norm_w"][j], p["dn_w_out"][j], p["ln_mix_g"][i], p["ln_mix_b"][i])
        x = moe_layer(x, p["moe_w_router"][i], p["moe_w_gate"][i], p["moe_w_up"][i], p["moe_w_down"][i],
                      p["ln_ffn_g"][i], p["ln_ffn_b"][i])
    return x


def kernel(x_prompt, x_sample, attn_w_qkv, attn_w_o, attn_sink, dn_w_in, dn_conv_w, dn_a_log, dn_dt_bias, dn_norm_w,
           dn_w_out, moe_w_router, moe_w_gate, moe_w_up, moe_w_down, ln_mix_g, ln_mix_b, ln_ffn_g, ln_ffn_b):
    c = lambda w: w.astype(MXU)
    p = dict(
        attn_w_qkv=c(attn_w_qkv), attn_w_o=c(attn_w_o), attn_sink=attn_sink,
        dn_w_conv=c(dn_w_in[:, :, :CONV_DIM]), dn_w_z=c(dn_w_in[:, :, CONV_DIM:CONV_DIM + V_DIM]),
        dn_w_ab=c(dn_w_in[:, :, CONV_DIM + V_DIM:]), dn_conv_w=dn_conv_w, dn_a_log=dn_a_log, dn_dt_bias=dn_dt_bias,
        dn_norm_w=dn_norm_w, dn_w_out=c(dn_w_out), moe_w_router=moe_w_router, moe_w_gate=c(moe_w_gate),
        moe_w_up=c(moe_w_up), moe_w_down=c(moe_w_down), ln_mix_g=ln_mix_g, ln_mix_b=ln_mix_b, ln_ffn_g=ln_ffn_g,
        ln_ffn_b=ln_ffn_b)
    return _trunk(x_prompt, p), _trunk(x_sample, p)
```

```python
import functools
import math

import jax
import jax.numpy as jnp
from jax import lax
from jax.experimental import pallas as pl
from jax.experimental.pallas import tpu as pltpu

f32, i32 = jnp.float32, jnp.int32
MXU = jnp.bfloat16

D_MODEL = 2048
DEPTH = 4
HQ, HKV, HD = 16, 4, 128
GROUP = HQ // HKV
BLOCK = 128
ROT_DIM = HD // 4
ROPE_THETA = 500000.0
HK, HV, DK, DV = 16, 32, 128, 128
QK_DIM, V_DIM = HK * DK, HV * DV
CONV_DIM = 2 * QK_DIM + V_DIM
CONV_K = 5
CHUNK = 64
N_EXPERTS = 16
D_EXPERT = 1024
CAP_FACTOR = 2
DEEPNORM_ALPHA = (2.0 * DEPTH) ** 0.25
LN_EPS = 1e-5
RMS_EPS = 1e-6

LANES = 128
V7X_VMEM_LIMIT = 56 * 1024 * 1024


def _params(*sem):
    return pltpu.CompilerParams(dimension_semantics=sem, vmem_limit_bytes=V7X_VMEM_LIMIT)


def _mm_kernel(a_ref, w_ref, o_ref):
    o_ref[...] = jnp.dot(a_ref[...].astype(MXU), w_ref[...], preferred_element_type=f32).astype(o_ref.dtype)


def matmul(a, w, out_dtype=f32, tm=1024, tn=1024):
    M, K = a.shape
    N = w.shape[1]
    tm, tn = min(tm, M), min(tn, N)
    assert M % tm == 0 and N % tn == 0
    return pl.pallas_call(
        _mm_kernel,
        grid=(N // tn, M // tm),
        in_specs=[pl.BlockSpec((tm, K), lambda j, i: (i, 0)), pl.BlockSpec((K, tn), lambda j, i: (0, j))],
        out_specs=pl.BlockSpec((tm, tn), lambda j, i: (i, j)),
        out_shape=jax.ShapeDtypeStruct((M, N), out_dtype),
        compiler_params=_params("parallel", "parallel"),
        name="matmul",
    )(a, w)


def _layer_norm_rows(y, g, b):
    mu = jnp.mean(y, axis=-1, keepdims=True)
    yc = y - mu
    var = jnp.mean(yc * yc, axis=-1, keepdims=True)
    return yc * lax.rsqrt(var + LN_EPS) * g + b


def _mm_ln_kernel(a_ref, w_ref, x_ref, g_ref, b_ref, o_ref):
    h = jnp.dot(a_ref[...].astype(MXU), w_ref[...], preferred_element_type=f32)
    o_ref[...] = _layer_norm_rows(DEEPNORM_ALPHA * x_ref[...] + h, g_ref[...], b_ref[...])


def matmul_res_ln(a, w, x, g, b, tm=256):
    M, K = a.shape
    D = w.shape[1]
    tm = min(tm, M)
    assert M % tm == 0
    return pl.pallas_call(
        _mm_ln_kernel,
        grid=(M // tm,),
        in_specs=[
            pl.BlockSpec((tm, K), lambda i: (i, 0)),
            pl.BlockSpec((K, D), lambda i: (0, 0), pipeline_mode=pl.Buffered(1)),
            pl.BlockSpec((tm, D), lambda i: (i, 0)),
            pl.BlockSpec((1, D), lambda i: (0, 0)),
            pl.BlockSpec((1, D), lambda i: (0, 0)),
        ],
        out_specs=pl.BlockSpec((tm, D), lambda i: (i, 0)),
        out_shape=jax.ShapeDtypeStruct((M, D), f32),
        compiler_params=_params("parallel"),
        name="matmul_res_ln",
    )(a, w, x, g.reshape(1, D), b.reshape(1, D))


def rope_table(T):
    half = ROT_DIM // 2
    inv = ROPE_THETA ** (-jnp.arange(0, ROT_DIM, 2, dtype=f32) / ROT_DIM)
    ang = jnp.arange(T, dtype=f32)[:, None] * inv[None, :]
    c, s = jnp.cos(ang), jnp.sin(ang)
    z = jnp.zeros((T, HD - ROT_DIM), f32)
    zh = jnp.zeros((T, half), f32)
    return jnp.concatenate([c, c, z + 1.0, -s, zh, z, zh, s, z], axis=1)


def _rope(t, tab):
    return t * tab[:, :HD] + pltpu.roll(t, HD - ROT_DIM // 2, 1) * tab[:, HD:2 * HD] + pltpu.roll(t, ROT_DIM // 2, 1) * tab[:, 2 * HD:]


def _attn_kernel(sink_ref, q_ref, kp_ref, kc_ref, kn_ref, vp_ref, vc_ref, vn_ref, tp_ref, tc_ref, tn_ref, o_ref):
    n = pl.program_id(1)
    nb = pl.num_programs(1)
    ii = lax.broadcasted_iota(i32, (GROUP * BLOCK, 3 * BLOCK), 0) % BLOCK
    jj = lax.broadcasted_iota(i32, (GROUP * BLOCK, 3 * BLOCK), 1)
    lo = jnp.where(n == 0, BLOCK, 0)
    hi = jnp.where(n == nb - 1, 2 * BLOCK, 3 * BLOCK)
    valid = (jj >= ii) & (jj <= ii + 2 * BLOCK) & (jj >= lo) & (jj < hi)
    tq = tc_ref[...]
    tabs = (tp_ref[...], tq, tn_ref[...])
    heads = range(HKV)
    hsl = [slice(h * HD, (h + 1) * HD) for h in heads]
    kbs = [jnp.concatenate([_rope(r[0, :, hs], t) for r, t in zip((kp_ref, kc_ref, kn_ref), tabs)], axis=0).astype(MXU)
           for hs in hsl]
    vbs = [jnp.concatenate([r[0, :, hs] for r in (vp_ref, vc_ref, vn_ref)], axis=0).astype(MXU) for hs in hsl]
    qhs = [jnp.concatenate([_rope(q_ref[0, :, (h * GROUP + g) * HD:(h * GROUP + g + 1) * HD], tq) for g in range(GROUP)],
                           axis=0).astype(MXU) for h in heads]
    ss = [jnp.where(valid, lax.dot_general(qh, kb, (((1,), (1,)), ((), ())), preferred_element_type=f32) * (HD ** -0.5),
                    -jnp.inf) for qh, kb in zip(qhs, kbs)]
    sks = [jnp.concatenate([jnp.full((BLOCK, 1), sink_ref[0, h * GROUP + g], f32) for g in range(GROUP)], axis=0)
           for h in heads]
    def lane_tiles(a):
        return [a[:, t * BLOCK:(t + 1) * BLOCK] for t in range(3)]

    ms = [jnp.maximum(jnp.max(functools.reduce(jnp.maximum, lane_tiles(s)), axis=-1, keepdims=True), sk)
          for s, sk in zip(ss, sks)]
    ps = [jnp.exp(s - m) for s, m in zip(ss, ms)]
    denoms = [jnp.sum(functools.reduce(jnp.add, lane_tiles(p)), axis=-1, keepdims=True) + jnp.exp(sk - m)
              for p, sk, m in zip(ps, sks, ms)]
    os_ = [jnp.dot((p / d).astype(MXU), vb, preferred_element_type=f32) for p, d, vb in zip(ps, denoms, vbs)]
    for h, o in zip(heads, os_):
        for g in range(GROUP):
            o_ref[0, :, (h * GROUP + g) * HD:(h * GROUP + g + 1) * HD] = o[g * BLOCK:(g + 1) * BLOCK].astype(o_ref.dtype)


def attention(qkv, sink):
    B, T, _ = qkv.shape
    out_dtype = MXU
    nb = T // BLOCK
    tab = rope_table(T)
    kcol, vcol = HQ // HKV, HQ // HKV + 1
    kvw = HKV * HD
    prev = lambda n: jnp.maximum(n - 1, 0)
    nxt = lambda n: jnp.minimum(n + 1, nb - 1)
    return pl.pallas_call(
        _attn_kernel,
        grid=(B, nb),
        in_specs=[
            pl.BlockSpec(memory_space=pltpu.SMEM),
            pl.BlockSpec((1, BLOCK, HQ * HD), lambda b, n: (b, n, 0)),
            pl.BlockSpec((1, BLOCK, kvw), lambda b, n: (b, prev(n), kcol)),
            pl.BlockSpec((1, BLOCK, kvw), lambda b, n: (b, n, kcol)),
            pl.BlockSpec((1, BLOCK, kvw), lambda b, n: (b, nxt(n), kcol)),
            pl.BlockSpec((1, BLOCK, kvw), lambda b, n: (b, prev(n), vcol)),
            pl.BlockSpec((1, BLOCK, kvw), lambda b, n: (b, n, vcol)),
            pl.BlockSpec((1, BLOCK, kvw), lambda b, n: (b, nxt(n), vcol)),
            pl.BlockSpec((BLOCK, 3 * HD), lambda b, n: (prev(n), 0)),
            pl.BlockSpec((BLOCK, 3 * HD), lambda b, n: (n, 0)),
            pl.BlockSpec((BLOCK, 3 * HD), lambda b, n: (nxt(n), 0)),
        ],
        out_specs=pl.BlockSpec((1, BLOCK, HQ * HD), lambda b, n: (b, n, 0)),
        out_shape=jax.ShapeDtypeStruct((B, T, HQ * HD), out_dtype),
        compiler_params=_params("parallel", "parallel"),
        name="attention",
    )(sink.reshape(1, HQ).astype(f32), qkv, qkv, qkv, qkv, qkv, qkv, qkv, tab, tab, tab)


def attn_layer(x, w_qkv, w_o, sink, ln_g, ln_b):
    B, T, D = x.shape
    x2 = x.reshape(B * T, D)
    qkv = matmul(x2, w_qkv).reshape(B, T, -1)
    o = attention(qkv, sink).reshape(B * T, HQ * HD)
    return matmul_res_ln(o, w_o, x2, ln_g, ln_b).reshape(B, T, D)


def _silu(x):
    return x * jax.nn.sigmoid(x)


def _conv_kernel(xp_ref, xc_ref, xn_ref, w_ref, o_ref, *, l2, scale):
    i = pl.program_id(1)
    tt = xc_ref.shape[1]
    pad = CONV_K // 2
    xp = jnp.where(i == 0, 0.0, xp_ref[0])
    xn = jnp.where(i == pl.num_programs(1) - 1, 0.0, xn_ref[0])
    xe = jnp.concatenate([xp, xc_ref[0], xn], axis=0)
    acc = xe[8 - pad:8 - pad + tt] * w_ref[0:1, :]
    for j in range(1, CONV_K):
        acc = acc + xe[8 - pad + j:8 - pad + j + tt] * w_ref[j:j + 1, :]
    y = _silu(acc)
    for h in range(y.shape[1] // DK):
        seg = y[:, h * DK:(h + 1) * DK]
        if l2:
            seg = seg * lax.rsqrt(jnp.sum(seg * seg, axis=-1, keepdims=True) + RMS_EPS) * scale
        o_ref[0, :, h * DK:(h + 1) * DK] = seg


def _conv_part(proj, conv_w, col0, ncols, l2, scale, tt=512, cb=512):
    B, T, _ = proj.shape
    tt = min(tt, T)
    assert T % tt == 0 and col0 % cb == 0 and ncols % cb == 0
    c0 = col0 // cb
    r8 = tt // 8
    return pl.pallas_call(
        functools.partial(_conv_kernel, l2=l2, scale=scale),
        grid=(B, T // tt, ncols // cb),
        in_specs=[
            pl.BlockSpec((1, 8, cb), lambda b, i, c: (b, jnp.maximum(i * r8 - 1, 0), c0 + c)),
            pl.BlockSpec((1, tt, cb), lambda b, i, c: (b, i, c0 + c)),
            pl.BlockSpec((1, 8, cb), lambda b, i, c: (b, jnp.minimum((i + 1) * r8, T // 8 - 1), c0 + c)),
            pl.BlockSpec((CONV_K, cb), lambda b, i, c: (0, c0 + c)),
        ],
        out_specs=pl.BlockSpec((1, tt, cb), lambda b, i, c: (b, i, c)),
        out_shape=jax.ShapeDtypeStruct((B, T, ncols), f32),
        compiler_params=_params("parallel", "parallel", "parallel"),
        name="dn_conv",
    )(proj, proj, proj, conv_w)


def conv_qkv(proj, conv_w):
    q = _conv_part(proj, conv_w, 0, QK_DIM, True, DK ** -0.5)
    k = _conv_part(proj, conv_w, QK_DIM, QK_DIM, True, 1.0)
    v = _conv_part(proj, conv_w, 2 * QK_DIM, V_DIM, False, 1.0)
    return q, k, v


def _gates_kernel(ab_ref, alog_ref, dtb_ref, o_ref):
    x = ab_ref[0]
    tt = x.shape[0]
    g = -jnp.exp(alog_ref[...]) * jax.nn.softplus(x + dtb_ref[...])
    beta = jax.nn.sigmoid(x)
    ri = lax.broadcasted_iota(i32, (tt, tt), 0)
    ci = lax.broadcasted_iota(i32, (tt, tt), 1)
    same = (ri // CHUNK) == (ci // CHUNK)
    lp = jnp.where(same & (ci <= ri), 1.0, 0.0).astype(f32)
    ls = jnp.where(same & (ci >= ri), 1.0, 0.0).astype(f32)
    gp = jnp.dot(lp, g, precision=lax.Precision.HIGHEST, preferred_element_type=f32)
    gs = jnp.dot(ls, g, precision=lax.Precision.HIGHEST, preferred_element_type=f32)
    lane = lax.broadcasted_iota(i32, x.shape, 1)
    cum = jnp.where(lane < 2 * HV, gp, gs)
    o_ref[0] = jnp.where((lane % (2 * HV)) < HV, cum, beta)


def gates(ab, a_log, dt_bias, tt=256):
    B, T, W = ab.shape
    tt = min(tt, T)
    z = jnp.zeros((2, HV), f32)
    alog = jnp.concatenate([a_log.astype(f32), z], axis=1).reshape(1, W)
    dtb = jnp.concatenate([dt_bias.astype(f32), z], axis=1).reshape(1, W)
    return pl.pallas_call(
        _gates_kernel,
        grid=(B, T // tt),
        in_specs=[pl.BlockSpec((1, tt, W), lambda b, i: (b, i, 0)), pl.BlockSpec((1, W), lambda b, i: (0, 0)),
                  pl.BlockSpec((1, W), lambda b, i: (0, 0))],
        out_specs=pl.BlockSpec((1, tt, W), lambda b, i: (b, i, 0)),
        out_shape=jax.ShapeDtypeStruct((B, T, W), f32),
        compiler_params=_params("parallel", "parallel"),
        name="dn_gates",
    )(ab, alog, dtb)


PACK = 4


def _mdot(a, b):
    return jnp.dot(a.astype(MXU), b.astype(MXU), preferred_element_type=f32)


def _packed_unit_tri_inverse(Ls, eye, bd_mask):
    c = Ls[0].shape[0]

    def bd(m):
        return jnp.where(bd_mask, jnp.concatenate([m] * PACK, axis=0), 0.0)

    ms = [-L for L in Ls]
    ps = [jnp.where(eye, 1.0, 0.0) + m for m in ms]
    ms = [_mdot(m, bd(m)) for m in ms]
    for _ in range(int(math.log2(c)) - 2):
        rs = [_mdot(jnp.concatenate([m, p], axis=0), bd(m)) for m, p in zip(ms, ps)]
        ms, ps = [r[:c] for r in rs], [p + r[c:] for p, r in zip(ps, rs)]
    return [p + _mdot(p, bd(m)) for m, p in zip(ms, ps)]


def _delta_kernel_packed(qf, kf, vf, gf, qb, kb, vb, gb, of, ob, s_ref, *, hb):
    C = CHUNK
    rep = HV // HK
    W = PACK * C
    h0 = pl.program_id(1) * hb

    @pl.when(pl.program_id(2) == 0)
    def _():
        s_ref[...] = jnp.zeros_like(s_ref)

    tb = qf.shape[1]
    nc = tb // C
    row = lax.broadcasted_iota(i32, (C, W), 0)
    col = lax.broadcasted_iota(i32, (C, W), 1) % C
    blk = lax.broadcasted_iota(i32, (C, W), 1) // C
    eye = row == col
    bd_mask = (lax.broadcasted_iota(i32, (W, W), 0) // C) == (lax.broadcasted_iota(i32, (W, W), 1) // C)
    blk2 = lax.broadcasted_iota(i32, (DK + C, W), 1) // C
    def side_by_side(parts):
        out = jnp.broadcast_to(parts[-1], (C, W))
        for b in range(PACK - 2, -1, -1):
            out = jnp.where(blk == b, parts[b], out)
        return out

    groups = []
    for reverse, (q_ref, k_ref, v_ref, g_ref) in enumerate(((qf, kf, vf, gf), (qb, kb, vb, gb))):
        incl = (row <= col) if reverse else (row >= col)
        strict = (row < col) if reverse else (row > col)
        gt = g_ref[0]
        lane = lax.broadcasted_iota(i32, gt.shape, 1)
        for h in range(hb):
            lane_g = [(2 * HV if reverse else 0) + rep * (h0 + h) + j for j in range(rep)]
            gcols = [jnp.sum(jnp.where(lane == lg, gt, 0.0), axis=1, keepdims=True) for lg in lane_g]
            bcols = [jnp.sum(jnp.where(lane == lg + HV, gt, 0.0), axis=1, keepdims=True) for lg in lane_g]
            for c0 in range(0, nc, PACK // rep):
                members = [(c0 + ci, j) for ci in range(PACK // rep) for j in range(rep)]
                kcs = {c: k_ref[0, c * C:(c + 1) * C, h * DK:(h + 1) * DK] for c, _ in members}
                qcs = {c: q_ref[0, c * C:(c + 1) * C, h * DK:(h + 1) * DK] for c, _ in members}
                gc = [gcols[j][c * C:(c + 1) * C] for c, j in members]
                bc = [bcols[j][c * C:(c + 1) * C] for c, j in members]
                g4, b4 = side_by_side(gc), side_by_side(bc)
                gram = jnp.concatenate(
                    [lax.dot_general(jnp.concatenate([kcs[c], qcs[c]], axis=0).astype(MXU),
                                     jnp.concatenate([kcs[c]] * rep, axis=0).astype(MXU), (((1,), (1,)), ((), ())),
                                     preferred_element_type=f32) for c in sorted(kcs)], axis=1)
                kk, qk = gram[:C], gram[C:]
                kT = jnp.concatenate([kcs[c] for c, _ in members], axis=0).T
                grow = jnp.sum(jnp.where(eye, g4, 0.0), axis=0, keepdims=True)
                glast = g4[0:1] if reverse else g4[C - 1:C]
                decay = jnp.exp(jnp.where(incl, g4 - grow, -jnp.inf))
                rhs = jnp.concatenate(
                    [jnp.concatenate([v_ref[0, c * C:(c + 1) * C, (h * rep + j) * DV:(h * rep + j + 1) * DV] * bc[b],
                                      kcs[c] * (bc[b] * jnp.exp(gc[b]))], axis=1) for b, (c, j) in enumerate(members)],
                    axis=0)
                groups.append(dict(
                    reverse=reverse, h=h, members=members, L=jnp.where(strict, b4 * kk * decay, 0.0), rhs=rhs,
                    lhs2=jnp.concatenate([kT * jnp.exp(glast - grow), qk * decay], axis=0),
                    qeg=[qcs[c] * jnp.exp(gc[b]) for b, (c, j) in enumerate(members)],
                    egl=[jnp.exp(gc[b][0:1] if reverse else gc[b][C - 1:C]) for b in range(PACK)]))
    tinvs = _packed_unit_tri_inverse([g["L"] for g in groups], eye, bd_mask)
    sols = [_mdot(jnp.where(bd_mask, jnp.concatenate([t] * PACK, axis=0), 0.0), g["rhs"]).astype(MXU)
            for t, g in zip(tinvs, groups)]
    local = {}
    for g, sol in zip(groups, sols):
        for b, (c, j) in enumerate(g["members"]):
            r2 = _mdot(jnp.where(blk2 == b, g["lhs2"], 0.0), sol)
            aq = jnp.concatenate([r2[:DK, DV:], g["qeg"][b] - r2[DK:, DV:]], axis=0).astype(MXU)
            local[(g["reverse"], g["h"], j, c)] = (aq, r2[:DK, :DV], r2[DK:, :DV], g["egl"][b])
    nch = 2 * hb * rep
    S = [s_ref[ch] for ch in range(nch)]
    for step in range(nc):
        for ch in range(nch):
            reverse, hcol = divmod(ch, hb * rep)
            h, j = divmod(hcol, rep)
            c = nc - 1 - step if reverse else step
            aq, bm, op, egl = local[(reverse, h, j, c)]
            r = jnp.dot(aq, S[ch].astype(MXU), preferred_element_type=f32)
            (ob if reverse else of)[0, c * C:(c + 1) * C, hcol * DV:(hcol + 1) * DV] = r[DK:] + op
            S[ch] = S[ch] * egl - r[:DK] + bm
    for ch in range(nch):
        s_ref[ch] = S[ch]


def delta_rule(q, k, v, gt, tb=256, hb=2):
    B, T, _ = q.shape
    tb = min(tb, T)
    nT = T // tb
    rep = HV // HK
    fwd = lambda b, h, i: (b, i, h)
    bwd = lambda b, h, i: (b, nT - 1 - i, h)
    fwd0 = lambda b, h, i: (b, i, 0)
    bwd0 = lambda b, h, i: (b, nT - 1 - i, 0)
    qk_spec = lambda m: pl.BlockSpec((1, tb, hb * DK), m)
    v_spec = lambda m: pl.BlockSpec((1, tb, hb * rep * DV), m)
    g_spec = lambda m: pl.BlockSpec((1, tb, gt.shape[2]), m)
    o_shape = jax.ShapeDtypeStruct((B, T, V_DIM), f32)
    return pl.pallas_call(
        functools.partial(_delta_kernel_packed, hb=hb),
        grid=(B, HK // hb, nT),
        in_specs=[qk_spec(fwd), qk_spec(fwd), v_spec(fwd), g_spec(fwd0), qk_spec(bwd), qk_spec(bwd), v_spec(bwd), g_spec(bwd0)],
        out_specs=[v_spec(fwd), v_spec(bwd)],
        out_shape=[o_shape, o_shape],
        scratch_shapes=[pltpu.VMEM((2 * hb * rep, DK, DV), f32)],
        compiler_params=_params("parallel", "parallel", "arbitrary"),
        name="dn_delta",
    )(q, k, v, gt, q, k, v, gt)


def _gnorm_kernel(of_ref, ob_ref, z_ref, nw_ref, o_ref):
    o = of_ref[...] + ob_ref[...]
    z = z_ref[...]
    for h in range(o.shape[1] // DV):
        hs = slice(h * DV, (h + 1) * DV)
        seg = o[:, hs]
        y = seg * lax.rsqrt(jnp.mean(seg * seg, axis=-1, keepdims=True) + RMS_EPS) * nw_ref[...]
        o_ref[:, hs] = (y * _silu(z[:, hs])).astype(o_ref.dtype)


def gated_norm(of, ob, z, norm_w, tt=256, cb=1024):
    M, W = of.shape
    tt = min(tt, M)
    spec = pl.BlockSpec((tt, cb), lambda i, c: (i, c))
    return pl.pallas_call(
        _gnorm_kernel,
        grid=(M // tt, W // cb),
        in_specs=[spec, spec, spec, pl.BlockSpec((1, DV), lambda i, c: (0, 0))],
        out_specs=spec,
        out_shape=jax.ShapeDtypeStruct((M, W), MXU),
        compiler_params=_params("parallel", "parallel"),
        name="dn_gated_norm",
    )(of, ob, z, norm_w.reshape(1, DV).astype(f32))


def dn_layer(x, w_conv, w_z, w_ab, conv_w, a_log, dt_bias, norm_w, w_out, ln_g, ln_b):
    B, T, D = x.shape
    x2 = x.reshape(B * T, D)
    pc = matmul(x2, w_conv).reshape(B, T, CONV_DIM)
    z = matmul(x2, w_z)
    ab = matmul(x2, w_ab).reshape(B, T, -1)
    q, k, v = conv_qkv(pc, conv_w)
    gt = gates(ab, a_log, dt_bias)
    of, ob = delta_rule(q, k, v, gt)
    o = gated_norm(of.reshape(B * T, V_DIM), ob.reshape(B * T, V_DIM), z, norm_w)
    return matmul_res_ln(o, w_out, x2, ln_g, ln_b).reshape(B, T, D)


EXACT = jnp.bfloat16


def _split3(x):
    hi = x.astype(MXU)
    return hi, (x - hi.astype(f32)).astype(MXU)


def _router_kernel(x_ref, w_ref, atok_ref, a2_ref):
    xh, xl = _split3(x_ref[...])
    wh, wl = _split3(w_ref[...])
    dot = functools.partial(jnp.dot, preferred_element_type=f32)
    logits = dot(xh, wh) + (dot(xl, wh) + dot(xh, wl))
    lane = lax.broadcasted_iota(i32, logits.shape, 1)
    lm = jnp.where(lane < N_EXPERTS, logits, -jnp.inf)
    e = jnp.exp(lm - jnp.max(lm, axis=-1, keepdims=True))
    aff = e / jnp.sum(e, axis=-1, keepdims=True)
    atok_ref[...] = aff
    for j in range(aff.shape[0] // LANES):
        bt = aff[j * LANES:(j + 1) * LANES, :].T
        for ex in range(N_EXPERTS):
            a2_ref[j:j + 1, ex * LANES:(ex + 1) * LANES] = bt[ex:ex + 1, :]


def router(x2, w_router, tm=1024):
    n, D = x2.shape
    tm = min(tm, n)
    w = jnp.zeros((D, LANES), f32).at[:, :N_EXPERTS].set(w_router.astype(f32))
    return pl.pallas_call(
        _router_kernel,
        grid=(n // tm,),
        in_specs=[pl.BlockSpec((tm, D), lambda i: (i, 0)), pl.BlockSpec((D, LANES), lambda i: (0, 0))],
        out_specs=[pl.BlockSpec((tm, LANES), lambda i: (i, 0)), pl.BlockSpec((tm // LANES, N_EXPERTS * LANES), lambda i: (i, 0))],
        out_shape=[jax.ShapeDtypeStruct((n, LANES), f32), jax.ShapeDtypeStruct((n // LANES, N_EXPERTS * LANES), f32)],
        compiler_params=_params("parallel"),
        name="moe_router",
    )(x2, w)


def _per_expert(row, op):
    parts = []
    for ex in range(N_EXPERTS):
        s = op(row[:, ex * LANES:(ex + 1) * LANES], axis=1, keepdims=True)
        parts.append(jnp.broadcast_to(s, (1, LANES)))
    return jnp.concatenate(parts, axis=1)


def _select_kernel(a_ref, idx_ref, dst_ref, tab_ref, kmax_ref, *, cap, n):
    R = a_ref.shape[0]
    W = N_EXPERTS * LANES
    bits = pltpu.bitcast(a_ref[...], i32)
    capf = jnp.float32(cap)

    def count(mask):
        return _per_expert(jnp.sum(jnp.where(mask, 1.0, 0.0), axis=0, keepdims=True), jnp.sum)

    def bit_step(t, ans):
        cand = ans | lax.shift_left(jnp.int32(1), 29 - t)
        return jnp.where(count(bits >= cand) >= capf, cand, ans)

    thr = lax.fori_loop(0, 30, bit_step, jnp.zeros((1, W), i32))
    gt = bits > thr
    eq = bits == thr
    need = capf - count(gt)

    lane_r = lax.broadcasted_iota(i32, (LANES, LANES), 0)
    lane_c = lax.broadcasted_iota(i32, (LANES, LANES), 1)
    upper = jnp.where(lane_r <= lane_c, 1.0, 0.0).astype(EXACT)
    row_r = lax.broadcasted_iota(i32, (R, R), 0)
    row_c = lax.broadcasted_iota(i32, (R, R), 1)
    below = jnp.where(row_c < row_r, 1.0, 0.0).astype(EXACT)

    def prefix(slab):
        lc = jnp.dot(slab.astype(EXACT), upper, preferred_element_type=f32)
        tot = jnp.broadcast_to(lc[:, LANES - 1:LANES], (R, LANES))
        off = jnp.dot(below, tot.astype(EXACT), preferred_element_type=f32)
        return lc, off, tot

    tok = (lax.broadcasted_iota(i32, (R, LANES), 0) * LANES + lax.broadcasted_iota(i32, (R, LANES), 1))
    sub_r = lax.broadcasted_iota(i32, (R, LANES), 0).astype(f32)
    sub_l = lax.broadcasted_iota(i32, (LANES, LANES), 0).astype(f32)
    lane_p = lax.broadcasted_iota(i32, (1, LANES), 1)
    nsel = jnp.zeros((R, LANES), f32)
    cut_parts = []
    for ex in range(N_EXPERTS):
        es = slice(ex * LANES, (ex + 1) * LANES)
        eq_e = jnp.where(eq[:, es], 1.0, 0.0)
        lc, off, _ = prefix(eq_e)
        take = eq[:, es] & (lc + off <= need[:, es])
        cut_parts.append(jnp.broadcast_to(jnp.max(jnp.max(jnp.where(take, tok, -1), axis=0, keepdims=True), axis=1, keepdims=True), (1, LANES)))
        sel_e = jnp.where(gt[:, es] | take, 1.0, 0.0)
        lc, off, tot = prefix(sel_e)
        rank_e = nsel
        nsel = nsel + sel_e
        hi = jnp.floor(off * (1.0 / LANES))
        lo = off - hi * LANES
        table = jnp.concatenate([lc.T, rank_e.T, hi.T[0:8], lo.T[0:8]], axis=0).astype(EXACT)
        end = off + tot

        def chunk(c, carry):
            c = lax.convert_element_type(c, i32)
            p =(c * LANES + lane_p).astype(f32)
            jp = jnp.sum(jnp.where(end <= p, 1.0, 0.0), axis=0, keepdims=True)
            onehot = jnp.where(sub_r == jp, 1.0, 0.0).astype(EXACT)
            g = jnp.dot(table, onehot, preferred_element_type=f32)
            p_loc = p - (g[2 * LANES:2 * LANES + 1] * LANES + g[2 * LANES + 8:2 * LANES + 9])
            lp = jnp.sum(jnp.where(g[:LANES] <= p_loc, 1.0, 0.0), axis=0, keepdims=True)
            rk = jnp.sum(jnp.where(sub_l == lp, g[LANES:2 * LANES], 0.0), axis=0, keepdims=True)
            t = jp * LANES + lp
            col = pl.ds(pl.multiple_of(c * LANES, LANES), LANES)
            idx_ref[ex:ex + 1, col] = t.astype(i32)
            dst_ref[ex:ex + 1, col] = (rk * n + t).astype(i32)
            return carry

        lax.fori_loop(0, cap // LANES, chunk, 0)

    diag = lax.broadcasted_iota(i32, (N_EXPERTS, LANES), 0) == lax.broadcasted_iota(i32, (N_EXPERTS, LANES), 1)
    thr_rows = jnp.concatenate([thr[:, ex * LANES:(ex + 1) * LANES] for ex in range(N_EXPERTS)], axis=0)
    cut_rows = jnp.concatenate(cut_parts, axis=0)
    tab_ref[...] = jnp.zeros_like(tab_ref)
    tab_ref[0:1, :] = jnp.sum(jnp.where(diag, thr_rows, 0), axis=0, keepdims=True)
    tab_ref[1:2, :] = jnp.sum(jnp.where(diag, cut_rows, 0), axis=0, keepdims=True)
    kmax_ref[...] = jnp.broadcast_to(jnp.max(nsel, axis=1, keepdims=True), (R, LANES)).astype(i32)


def select(a2, cap):
    R = a2.shape[0]
    n = R * LANES
    assert cap % LANES == 0
    return pl.pallas_call(
        functools.partial(_select_kernel, cap=cap, n=n),
        out_shape=[jax.ShapeDtypeStruct((N_EXPERTS, cap), i32), jax.ShapeDtypeStruct((N_EXPERTS, cap), i32),
                   jax.ShapeDtypeStruct((8, LANES), i32), jax.ShapeDtypeStruct((R, LANES), i32)],
        compiler_params=pltpu.CompilerParams(vmem_limit_bytes=V7X_VMEM_LIMIT),
        name="moe_select",
    )(a2)


ROW_DMA_UNROLL = 8

def _ffn_kernel(idx_ref, nidx_ref, dst_ref, x_hbm, wg_ref, wu_ref, wd_ref, y_hbm, xbuf, ybuf, gsem, ssem):
    tm = xbuf.shape[1]
    s = pl.program_id(0) * pl.num_programs(1) + pl.program_id(1)
    last = pl.num_programs(0) * pl.num_programs(1) - 1
    slot = s % 2

    def gather(ref, r, sl):
        return pltpu.make_async_copy(x_hbm.at[pl.ds(ref[0, 0, r], 1), :], xbuf.at[sl, pl.ds(r, 1), :], gsem.at[sl])

    def scatter(r, sl):
        return pltpu.make_async_copy(ybuf.at[sl, pl.ds(r, 1), :], y_hbm.at[pl.ds(dst_ref[0, 0, r], 1), :], ssem.at[sl])

    def each_row(fn):
        def body(r, c):
            fn(r)
            return c
        lax.fori_loop(0, tm, body, 0, unroll=ROW_DMA_UNROLL)

    @pl.when(s == 0)
    def _():
        each_row(lambda r: gather(idx_ref, r, 0).start())

    each_row(lambda r: gather(idx_ref, r, slot).wait())

    @pl.when(s >= 2)
    def _():
        each_row(lambda r: scatter(r, slot).wait())

    for r in range(tm):
        gather(nidx_ref, r, 1 - slot).start()
    xe = xbuf[slot].astype(MXU)
    gate = jnp.dot(xe, wg_ref[...], preferred_element_type=f32)
    up = jnp.dot(xe, wu_ref[...], preferred_element_type=f32)
    h = (_silu(gate) * up).astype(MXU)
    ybuf[slot] = jnp.dot(h, wd_ref[...], preferred_element_type=f32)
    for r in range(tm):
        scatter(r, slot).start()

    @pl.when(s == last)
    def _():
        each_row(lambda r: gather(nidx_ref, r, 1 - slot).wait())

        @pl.when(s >= 1)
        def _():
            each_row(lambda r: scatter(r, 1 - slot).wait())

        each_row(lambda r: scatter(r, slot).wait())


def moe_ffn(x2, idx, dst, w_gate, w_up, w_down, tm=256):
    n, D = x2.shape
    E, cap = idx.shape
    F = w_gate.shape[2]
    tm = min(tm, cap)
    nc = cap // tm
    ispec = pl.BlockSpec((1, 1, tm), lambda e, c: (e * nc + c, 0, 0), memory_space=pltpu.SMEM)
    nspec = pl.BlockSpec((1, 1, tm), lambda e, c: (jnp.minimum(e * nc + c + 1, E * nc - 1), 0, 0), memory_space=pltpu.SMEM)
    idx3 = idx.reshape(E * nc, 1, tm)
    return pl.pallas_call(
        _ffn_kernel,
        grid=(E, nc),
        in_specs=[
            ispec, nspec, ispec,
            pl.BlockSpec(memory_space=pl.ANY),
            pl.BlockSpec((None, D, F), lambda e, c: (e, 0, 0)),
            pl.BlockSpec((None, D, F), lambda e, c: (e, 0, 0)),
            pl.BlockSpec((None, F, D), lambda e, c: (e, 0, 0)),
        ],
        out_specs=pl.BlockSpec(memory_space=pl.ANY),
        out_shape=jax.ShapeDtypeStruct((E * n, D), f32),
        scratch_shapes=[pltpu.VMEM((2, tm, D), f32), pltpu.VMEM((2, tm, D), f32), pltpu.SemaphoreType.DMA((2,)),
                        pltpu.SemaphoreType.DMA((2,))],
        compiler_params=_params("arbitrary", "arbitrary"),
        name="moe_ffn",
    )(idx3, idx3, dst.reshape(E * nc, 1, tm), x2, w_gate, w_up, w_down)


COMBINE_COLS = 256


def _combine_kernel(kmax_ref, y_hbm, atok_ref, tab_ref, x_ref, g_ref, b_ref, o_ref, ybuf, gbuf, hbuf, sem):
    i = pl.program_id(0)
    nt = pl.num_programs(0)
    tt, D = x_ref.shape
    half = i % 2
    km = kmax_ref[i]

    def fetch(tile, k, hf):
        return pltpu.make_async_copy(y_hbm.at[pl.ds(k * (nt * tt) + tile * tt, tt), :], ybuf.at[hf, k], sem.at[hf, k])

    def each_slot(count, fn):
        def body(k, c):
            fn(k)
            return c
        lax.fori_loop(0, count, body, 0)

    @pl.when(i == 0)
    def _():
        each_slot(km, lambda k: fetch(0, k, 0).start())

    @pl.when(i + 1 < nt)
    def _():
        each_slot(kmax_ref[jnp.minimum(i + 1, nt - 1)], lambda k: fetch(i + 1, k, 1 - half).start())

    aff = atok_ref[...]
    bits = pltpu.bitcast(aff, i32)
    lane = lax.broadcasted_iota(i32, aff.shape, 1)
    tok = i * tt + lax.broadcasted_iota(i32, aff.shape, 0)
    thr, cut = tab_ref[0:1, :], tab_ref[1:2, :]
    sel = (lane < N_EXPERTS) & ((bits > thr) | ((bits == thr) & (tok <= cut)))
    before = jnp.where(lax.broadcasted_iota(i32, (LANES, LANES), 0) < lax.broadcasted_iota(i32, (LANES, LANES), 1), 1.0, 0.0)
    rank = jnp.dot(jnp.where(sel, 1.0, 0.0).astype(EXACT), before.astype(EXACT), preferred_element_type=f32)

    def slot_gate(k):
        mine = sel & (rank == lax.convert_element_type(k, f32))
        gk = jnp.sum(jnp.where(mine, aff, 0.0), axis=1, keepdims=True)
        has = jnp.sum(jnp.where(mine, 1.0, 0.0), axis=1, keepdims=True) > 0.0
        gbuf[k] = jnp.broadcast_to(jnp.where(has, gk, -1.0), (tt, LANES))

    each_slot(km, slot_gate)
    each_slot(km, lambda k: fetch(i, k, half).wait())
    reps = COMBINE_COLS // LANES
    for cc in range(D // COMBINE_COLS):
        cs = slice(cc * COMBINE_COLS, (cc + 1) * COMBINE_COLS)

        def add_slot(k, acc):
            gk = jnp.concatenate([gbuf[k]] * reps, axis=1)
            return acc + jnp.where(gk >= 0.0, ybuf[half, k, :, cs], 0.0) * gk

        hbuf[:, cs] = lax.fori_loop(0, km, add_slot, jnp.zeros((tt, COMBINE_COLS), f32))
    o_ref[...] = _layer_norm_rows(DEEPNORM_ALPHA * x_ref[...] + hbuf[...], g_ref[...], b_ref[...])


def moe_combine(y, atok, tab, kmax, x2, g, b, tt=LANES):
    n, D = x2.shape
    assert tt == LANES
    return pl.pallas_call(
        _combine_kernel,
        grid_spec=pltpu.PrefetchScalarGridSpec(
            num_scalar_prefetch=1,
            grid=(n // tt,),
            in_specs=[
                pl.BlockSpec(memory_space=pl.ANY),
                pl.BlockSpec((tt, LANES), lambda i, km: (i, 0)),
                pl.BlockSpec((8, LANES), lambda i, km: (0, 0)),
                pl.BlockSpec((tt, D), lambda i, km: (i, 0)),
                pl.BlockSpec((1, D), lambda i, km: (0, 0)),
                pl.BlockSpec((1, D), lambda i, km: (0, 0)),
            ],
            out_specs=pl.BlockSpec((tt, D), lambda i, km: (i, 0)),
            scratch_shapes=[pltpu.VMEM((2, N_EXPERTS, tt, D), f32), pltpu.VMEM((N_EXPERTS, tt, LANES), f32),
                            pltpu.VMEM((tt, D), f32), pltpu.SemaphoreType.DMA((2, N_EXPERTS))],
        ),
        out_shape=jax.ShapeDtypeStruct((n, D), f32),
        compiler_params=_params("arbitrary"),
        name="moe_combine",
    )(kmax, y, atok, tab, x2, g.reshape(1, D), b.reshape(1, D))


def moe_layer(x, w_router, w_gate, w_up, w_down, ln_g, ln_b):
    B, T, D = x.shape
    n = B * T
    x2 = x.reshape(n, D)
    cap = CAP_FACTOR * n // N_EXPERTS
    atok, a2 = router(x2, w_router)
    idx, dst, tab, kmax = select(a2, cap)
    y = moe_ffn(x2, idx, dst, w_gate, w_up, w_down)
    return moe_combine(y, atok, tab, kmax[:, 0], x2, ln_g, ln_b).reshape(B, T, D)


def _trunk(x, p):
    for i in range(DEPTH):
        j = i // 2
        if i % 2 == 0:
            x = attn_layer(x, p["attn_w_qkv"][j], p["attn_w_o"][j], p["attn_sink"][j], p["ln_mix_g"][i], p["ln_mix_b"][i])
        else:
            x = dn_layer(x, p["dn_w_conv"][j], p["dn_w_z"][j], p["dn_w_ab"][j], p["dn_conv_w"][j], p["dn_a_log"][j],
                         p["dn_dt_bias"][j], p["dn_norm_w"][j], p["dn_w_out"][j], p["ln_mix_g"][i], p["ln_mix_b"][i])
        x = moe_layer(x, p["moe_w_router"][i], p["moe_w_gate"][i], p["moe_w_up"][i], p["moe_w_down"][i],
                      p["ln_ffn_g"][i], p["ln_ffn_b"][i])
    return x


def kernel(x_prompt, x_sample, attn_w_qkv, attn_w_o, attn_sink, dn_w_in, dn_conv_w, dn_a_log, dn_dt_bias, dn_norm_w,
           dn_w_out, moe_w_router, moe_w_gate, moe_w_up, moe_w_down, ln_mix_g, ln_mix_b, ln_ffn_g, ln_ffn_b):
    c = lambda w: w.astype(MXU)
    p = dict(
        attn_w_qkv=c(attn_w_qkv), attn_w_o=c(attn_w_o), attn_sink=attn_sink,
        dn_w_conv=c(dn_w_in[:, :, :CONV_DIM]), dn_w_z=c(dn_w_in[:, :, CONV_DIM:CONV_DIM + V_DIM]),
        dn_w_ab=c(dn_w_in[:, :, CONV_DIM + V_DIM:]), dn_conv_w=dn_conv_w, dn_a_log=dn_a_log, dn_dt_bias=dn_dt_bias,
        dn_norm_w=dn_norm_w, dn_w_out=c(dn_w_out), moe_w_router=moe_w_router, moe_w_gate=c(moe_w_gate),
        moe_w_up=c(moe_w_up), moe_w_down=c(moe_w_down), ln_mix_g=ln_mix_g, ln_mix_b=ln_mix_b, ln_ffn_g=ln_ffn_g,
        ln_ffn_b=ln_ffn_b)
    return _trunk(x_prompt, p), _trunk(x_sample, p)
```

```python
import functools
import math

import jax
import jax.numpy as jnp
from jax import lax
from jax.experimental import pallas as pl
from jax.experimental.pallas import tpu as pltpu

f32, i32 = jnp.float32, jnp.int32
MXU = jnp.bfloat16

D_MODEL = 2048
DEPTH = 4
HQ, HKV, HD = 16, 4, 128
GROUP = HQ // HKV
BLOCK = 128
ROT_DIM = HD // 4
ROPE_THETA = 500000.0
HK, HV, DK, DV = 16, 32, 128, 128
QK_DIM, V_DIM = HK * DK, HV * DV
CONV_DIM = 2 * QK_DIM + V_DIM
CONV_K = 5
CHUNK = 64
N_EXPERTS = 16
D_EXPERT = 1024
CAP_FACTOR = 2
DEEPNORM_ALPHA = (2.0 * DEPTH) ** 0.25
LN_EPS = 1e-5
RMS_EPS = 1e-6

LANES = 128
V7X_VMEM_LIMIT = 56 * 1024 * 1024


def _params(*sem):
    return pltpu.CompilerParams(dimension_semantics=sem, vmem_limit_bytes=V7X_VMEM_LIMIT)


def _mm_kernel(a_ref, w_ref, o_ref):
    o_ref[...] = jnp.dot(a_ref[...].astype(MXU), w_ref[...], preferred_element_type=f32).astype(o_ref.dtype)


def matmul(a, w, out_dtype=f32, tm=1024, tn=1024):
    M, K = a.shape
    N = w.shape[1]
    tm, tn = min(tm, M), min(tn, N)
    assert M % tm == 0 and N % tn == 0
    return pl.pallas_call(
        _mm_kernel,
        grid=(N // tn, M // tm),
        in_specs=[pl.BlockSpec((tm, K), lambda j, i: (i, 0)), pl.BlockSpec((K, tn), lambda j, i: (0, j))],
        out_specs=pl.BlockSpec((tm, tn), lambda j, i: (i, j)),
        out_shape=jax.ShapeDtypeStruct((M, N), out_dtype),
        compiler_params=_params("parallel", "parallel"),
        name="matmul",
    )(a, w)


def _layer_norm_rows(y, g, b):
    mu = jnp.mean(y, axis=-1, keepdims=True)
    yc = y - mu
    var = jnp.mean(yc * yc, axis=-1, keepdims=True)
    return yc * lax.rsqrt(var + LN_EPS) * g + b


def _mm_ln_kernel(a_ref, w_ref, x_ref, g_ref, b_ref, o_ref):
    h = jnp.dot(a_ref[...].astype(MXU), w_ref[...], preferred_element_type=f32)
    o_ref[...] = _layer_norm_rows(DEEPNORM_ALPHA * x_ref[...] + h, g_ref[...], b_ref[...])


def matmul_res_ln(a, w, x, g, b, tm=256):
    M, K = a.shape
    D = w.shape[1]
    tm = min(tm, M)
    assert M % tm == 0
    return pl.pallas_call(
        _mm_ln_kernel,
        grid=(M // tm,),
        in_specs=[
            pl.BlockSpec((tm, K), lambda i: (i, 0)),
            pl.BlockSpec((K, D), lambda i: (0, 0), pipeline_mode=pl.Buffered(1)),
            pl.BlockSpec((tm, D), lambda i: (i, 0)),
            pl.BlockSpec((1, D), lambda i: (0, 0)),
            pl.BlockSpec((1, D), lambda i: (0, 0)),
        ],
        out_specs=pl.BlockSpec((tm, D), lambda i: (i, 0)),
        out_shape=jax.ShapeDtypeStruct((M, D), f32),
        compiler_params=_params("parallel"),
        name="matmul_res_ln",
    )(a, w, x, g.reshape(1, D), b.reshape(1, D))


def rope_table(T):
    half = ROT_DIM // 2
    inv = ROPE_THETA ** (-jnp.arange(0, ROT_DIM, 2, dtype=f32) / ROT_DIM)
    ang = jnp.arange(T, dtype=f32)[:, None] * inv[None, :]
    c, s = jnp.cos(ang), jnp.sin(ang)
    z = jnp.zeros((T, HD - ROT_DIM), f32)
    zh = jnp.zeros((T, half), f32)
    return jnp.concatenate([c, c, z + 1.0, -s, zh, z, zh, s, z], axis=1)


def _rope(t, tab):
    return t * tab[:, :HD] + pltpu.roll(t, HD - ROT_DIM // 2, 1) * tab[:, HD:2 * HD] + pltpu.roll(t, ROT_DIM // 2, 1) * tab[:, 2 * HD:]


def _attn_kernel(sink_ref, q_ref, kp_ref, kc_ref, kn_ref, vp_ref, vc_ref, vn_ref, tp_ref, tc_ref, tn_ref, o_ref):
    n = pl.program_id(1)
    nb = pl.num_programs(1)
    ii = lax.broadcasted_iota(i32, (GROUP * BLOCK, 3 * BLOCK), 0) % BLOCK
    jj = lax.broadcasted_iota(i32, (GROUP * BLOCK, 3 * BLOCK), 1)
    lo = jnp.where(n == 0, BLOCK, 0)
    hi = jnp.where(n == nb - 1, 2 * BLOCK, 3 * BLOCK)
    valid = (jj >= ii) & (jj <= ii + 2 * BLOCK) & (jj >= lo) & (jj < hi)
    tq = tc_ref[...]
    tabs = (tp_ref[...], tq, tn_ref[...])
    heads = range(HKV)
    hsl = [slice(h * HD, (h + 1) * HD) for h in heads]
    kbs = [jnp.concatenate([_rope(r[0, :, hs], t) for r, t in zip((kp_ref, kc_ref, kn_ref), tabs)], axis=0).astype(MXU)
           for hs in hsl]
    vbs = [jnp.concatenate([r[0, :, hs] for r in (vp_ref, vc_ref, vn_ref)], axis=0).astype(MXU) for hs in hsl]
    qhs = [jnp.concatenate([_rope(q_ref[0, :, (h * GROUP + g) * HD:(h * GROUP + g + 1) * HD], tq) for g in range(GROUP)],
                           axis=0).astype(MXU) for h in heads]
    ss = [jnp.where(valid, lax.dot_general(qh, kb, (((1,), (1,)), ((), ())), preferred_element_type=f32) * (HD ** -0.5),
                    -jnp.inf) for qh, kb in zip(qhs, kbs)]
    sks = [jnp.concatenate([jnp.full((BLOCK, 1), sink_ref[0, h * GROUP + g], f32) for g in range(GROUP)], axis=0)
           for h in heads]
    def lane_tiles(a):
        return [a[:, t * BLOCK:(t + 1) * BLOCK] for t in range(3)]

    ms = [jnp.maximum(jnp.max(functools.reduce(jnp.maximum, lane_tiles(s)), axis=-1, keepdims=True), sk)
          for s, sk in zip(ss, sks)]
    ps = [jnp.exp(s - m) for s, m in zip(ss, ms)]
    denoms = [jnp.sum(functools.reduce(jnp.add, lane_tiles(p)), axis=-1, keepdims=True) + jnp.exp(sk - m)
              for p, sk, m in zip(ps, sks, ms)]
    os_ = [jnp.dot((p / d).astype(MXU), vb, preferred_element_type=f32) for p, d, vb in zip(ps, denoms, vbs)]
    for h, o in zip(heads, os_):
        for g in range(GROUP):
            o_ref[0, :, (h * GROUP + g) * HD:(h * GROUP + g + 1) * HD] = o[g * BLOCK:(g + 1) * BLOCK].astype(o_ref.dtype)


def attention(qkv, sink):
    B, T, _ = qkv.shape
    out_dtype = MXU
    nb = T // BLOCK
    tab = rope_table(T)
    kcol, vcol = HQ // HKV, HQ // HKV + 1
    kvw = HKV * HD
    prev = lambda n: jnp.maximum(n - 1, 0)
    nxt = lambda n: jnp.minimum(n + 1, nb - 1)
    return pl.pallas_call(
        _attn_kernel,
        grid=(B, nb),
        in_specs=[
            pl.BlockSpec(memory_space=pltpu.SMEM),
            pl.BlockSpec((1, BLOCK, HQ * HD), lambda b, n: (b, n, 0)),
            pl.BlockSpec((1, BLOCK, kvw), lambda b, n: (b, prev(n), kcol)),
            pl.BlockSpec((1, BLOCK, kvw), lambda b, n: (b, n, kcol)),
            pl.BlockSpec((1, BLOCK, kvw), lambda b, n: (b, nxt(n), kcol)),
            pl.BlockSpec((1, BLOCK, kvw), lambda b, n: (b, prev(n), vcol)),
            pl.BlockSpec((1, BLOCK, kvw), lambda b, n: (b, n, vcol)),
            pl.BlockSpec((1, BLOCK, kvw), lambda b, n: (b, nxt(n), vcol)),
            pl.BlockSpec((BLOCK, 3 * HD), lambda b, n: (prev(n), 0)),
            pl.BlockSpec((BLOCK, 3 * HD), lambda b, n: (n, 0)),
            pl.BlockSpec((BLOCK, 3 * HD), lambda b, n: (nxt(n), 0)),
        ],
        out_specs=pl.BlockSpec((1, BLOCK, HQ * HD), lambda b, n: (b, n, 0)),
        out_shape=jax.ShapeDtypeStruct((B, T, HQ * HD), out_dtype),
        compiler_params=_params("parallel", "parallel"),
        name="attention",
    )(sink.reshape(1, HQ).astype(f32), qkv, qkv, qkv, qkv, qkv, qkv, qkv, tab, tab, tab)


def attn_layer(x, w_qkv, w_o, sink, ln_g, ln_b):
    B, T, D = x.shape
    x2 = x.reshape(B * T, D)
    qkv = matmul(x2, w_qkv).reshape(B, T, -1)
    o = attention(qkv, sink).reshape(B * T, HQ * HD)
    return matmul_res_ln(o, w_o, x2, ln_g, ln_b).reshape(B, T, D)


def _silu(x):
    return x * jax.nn.sigmoid(x)


def _conv_kernel(xp_ref, xc_ref, xn_ref, w_ref, o_ref, *, l2, scale):
    i = pl.program_id(1)
    tt = xc_ref.shape[1]
    pad = CONV_K // 2
    xp = jnp.where(i == 0, 0.0, xp_ref[0])
    xn = jnp.where(i == pl.num_programs(1) - 1, 0.0, xn_ref[0])
    xe = jnp.concatenate([xp, xc_ref[0], xn], axis=0)
    acc = xe[8 - pad:8 - pad + tt] * w_ref[0:1, :]
    for j in range(1, CONV_K):
        acc = acc + xe[8 - pad + j:8 - pad + j + tt] * w_ref[j:j + 1, :]
    y = _silu(acc)
    for h in range(y.shape[1] // DK):
        seg = y[:, h * DK:(h + 1) * DK]
        if l2:
            seg = seg * lax.rsqrt(jnp.sum(seg * seg, axis=-1, keepdims=True) + RMS_EPS) * scale
        o_ref[0, :, h * DK:(h + 1) * DK] = seg


def _conv_part(proj, conv_w, col0, ncols, l2, scale, tt=512, cb=512):
    B, T, _ = proj.shape
    tt = min(tt, T)
    assert T % tt == 0 and col0 % cb == 0 and ncols % cb == 0
    c0 = col0 // cb
    r8 = tt // 8
    return pl.pallas_call(
        functools.partial(_conv_kernel, l2=l2, scale=scale),
        grid=(B, T // tt, ncols // cb),
        in_specs=[
            pl.BlockSpec((1, 8, cb), lambda b, i, c: (b, jnp.maximum(i * r8 - 1, 0), c0 + c)),
            pl.BlockSpec((1, tt, cb), lambda b, i, c: (b, i, c0 + c)),
            pl.BlockSpec((1, 8, cb), lambda b, i, c: (b, jnp.minimum((i + 1) * r8, T // 8 - 1), c0 + c)),
            pl.BlockSpec((CONV_K, cb), lambda b, i, c: (0, c0 + c)),
        ],
        out_specs=pl.BlockSpec((1, tt, cb), lambda b, i, c: (b, i, c)),
        out_shape=jax.ShapeDtypeStruct((B, T, ncols), f32),
        compiler_params=_params("parallel", "parallel", "parallel"),
        name="dn_conv",
    )(proj, proj, proj, conv_w)


def conv_qkv(proj, conv_w):
    q = _conv_part(proj, conv_w, 0, QK_DIM, True, DK ** -0.5)
    k = _conv_part(proj, conv_w, QK_DIM, QK_DIM, True, 1.0)
    v = _conv_part(proj, conv_w, 2 * QK_DIM, V_DIM, False, 1.0)
    return q, k, v


def _gates_kernel(ab_ref, alog_ref, dtb_ref, o_ref):
    x = ab_ref[0]
    tt = x.shape[0]
    g = -jnp.exp(alog_ref[...]) * jax.nn.softplus(x + dtb_ref[...])
    beta = jax.nn.sigmoid(x)
    ri = lax.broadcasted_iota(i32, (tt, tt), 0)
    ci = lax.broadcasted_iota(i32, (tt, tt), 1)
    same = (ri // CHUNK) == (ci // CHUNK)
    lp = jnp.where(same & (ci <= ri), 1.0, 0.0).astype(f32)
    ls = jnp.where(same & (ci >= ri), 1.0, 0.0).astype(f32)
    gp = jnp.dot(lp, g, precision=lax.Precision.HIGHEST, preferred_element_type=f32)
    gs = jnp.dot(ls, g, precision=lax.Precision.HIGHEST, preferred_element_type=f32)
    lane = lax.broadcasted_iota(i32, x.shape, 1)
    cum = jnp.where(lane < 2 * HV, gp, gs)
    o_ref[0] = jnp.where((lane % (2 * HV)) < HV, cum, beta)


def gates(ab, a_log, dt_bias, tt=256):
    B, T, W = ab.shape
    tt = min(tt, T)
    z = jnp.zeros((2, HV), f32)
    alog = jnp.concatenate([a_log.astype(f32), z], axis=1).reshape(1, W)
    dtb = jnp.concatenate([dt_bias.astype(f32), z], axis=1).reshape(1, W)
    return pl.pallas_call(
        _gates_kernel,
        grid=(B, T // tt),
        in_specs=[pl.BlockSpec((1, tt, W), lambda b, i: (b, i, 0)), pl.BlockSpec((1, W), lambda b, i: (0, 0)),
                  pl.BlockSpec((1, W), lambda b, i: (0, 0))],
        out_specs=pl.BlockSpec((1, tt, W), lambda b, i: (b, i, 0)),
        out_shape=jax.ShapeDtypeStruct((B, T, W), f32),
        compiler_params=_params("parallel", "parallel"),
        name="dn_gates",
    )(ab, alog, dtb)


PACK = 4


def _mdot(a, b):
    return jnp.dot(a.astype(MXU), b.astype(MXU), preferred_element_type=f32)


def _packed_unit_tri_inverse(Ls, eye, bd_mask):
    c = Ls[0].shape[0]

    def bd(m):
        return jnp.where(bd_mask, jnp.concatenate([m] * PACK, axis=0), 0.0)

    ms = [-L for L in Ls]
    ps = [jnp.where(eye, 1.0, 0.0) + m for m in ms]
    ms = [_mdot(m, bd(m)) for m in ms]
    for _ in range(int(math.log2(c)) - 2):
        rs = [_mdot(jnp.concatenate([m, p], axis=0), bd(m)) for m, p in zip(ms, ps)]
        ms, ps = [r[:c] for r in rs], [p + r[c:] for p, r in zip(ps, rs)]
    return [p + _mdot(p, bd(m)) for m, p in zip(ms, ps)]


def _delta_kernel_packed(qf, kf, vf, gf, qb, kb, vb, gb, of, ob, s_ref, *, hb):
    C = CHUNK
    rep = HV // HK
    W = PACK * C
    h0 = pl.program_id(1) * hb

    @pl.when(pl.program_id(2) == 0)
    def _():
        s_ref[...] = jnp.zeros_like(s_ref)

    tb = qf.shape[1]
    nc = tb // C
    row = lax.broadcasted_iota(i32, (C, W), 0)
    col = lax.broadcasted_iota(i32, (C, W), 1) % C
    blk = lax.broadcasted_iota(i32, (C, W), 1) // C
    eye = row == col
    bd_mask = (lax.broadcasted_iota(i32, (W, W), 0) // C) == (lax.broadcasted_iota(i32, (W, W), 1) // C)
    blk2 = lax.broadcasted_iota(i32, (DK + C, W), 1) // C
    def side_by_side(parts):
        out = jnp.broadcast_to(parts[-1], (C, W))
        for b in range(PACK - 2, -1, -1):
            out = jnp.where(blk == b, parts[b], out)
        return out

    groups = []
    for reverse, (q_ref, k_ref, v_ref, g_ref) in enumerate(((qf, kf, vf, gf), (qb, kb, vb, gb))):
        incl = (row <= col) if reverse else (row >= col)
        strict = (row < col) if reverse else (row > col)
        gt = g_ref[0]
        lane = lax.broadcasted_iota(i32, gt.shape, 1)
        for h in range(hb):
            lane_g = [(2 * HV if reverse else 0) + rep * (h0 + h) + j for j in range(rep)]
            gcols = [jnp.sum(jnp.where(lane == lg, gt, 0.0), axis=1, keepdims=True) for lg in lane_g]
            bcols = [jnp.sum(jnp.where(lane == lg + HV, gt, 0.0), axis=1, keepdims=True) for lg in lane_g]
            for c0 in range(0, nc, PACK // rep):
                members = [(c0 + ci, j) for ci in range(PACK // rep) for j in range(rep)]
                kcs = {c: k_ref[0, c * C:(c + 1) * C, h * DK:(h + 1) * DK] for c, _ in members}
                qcs = {c: q_ref[0, c * C:(c + 1) * C, h * DK:(h + 1) * DK] for c, _ in members}
                gc = [gcols[j][c * C:(c + 1) * C] for c, j in members]
                bc = [bcols[j][c * C:(c + 1) * C] for c, j in members]
                g4, b4 = side_by_side(gc), side_by_side(bc)
                gram = jnp.concatenate(
                    [lax.dot_general(jnp.concatenate([kcs[c], qcs[c]], axis=0).astype(MXU),
                                     jnp.concatenate([kcs[c]] * rep, axis=0).astype(MXU), (((1,), (1,)), ((), ())),
                                     preferred_element_type=f32) for c in sorted(kcs)], axis=1)
                kk, qk = gram[:C], gram[C:]
                kT = jnp.concatenate([kcs[c] for c, _ in members], axis=0).T
                grow = jnp.sum(jnp.where(eye, g4, 0.0), axis=0, keepdims=True)
                glast = g4[0:1] if reverse else g4[C - 1:C]
                decay = jnp.exp(jnp.where(incl, g4 - grow, -jnp.inf))
                rhs = jnp.concatenate(
                    [jnp.concatenate([v_ref[0, c * C:(c + 1) * C, (h * rep + j) * DV:(h * rep + j + 1) * DV] * bc[b],
                                      kcs[c] * (bc[b] * jnp.exp(gc[b]))], axis=1) for b, (c, j) in enumerate(members)],
                    axis=0)
                groups.append(dict(
                    reverse=reverse, h=h, members=members, L=jnp.where(strict, b4 * kk * decay, 0.0), rhs=rhs,
                    lhs2=jnp.concatenate([kT * jnp.exp(glast - grow), qk * decay], axis=0),
                    qeg=[qcs[c] * jnp.exp(gc[b]) for b, (c, j) in enumerate(members)],
                    egl=[jnp.exp(gc[b][0:1] if reverse else gc[b][C - 1:C]) for b in range(PACK)]))
    tinvs = _packed_unit_tri_inverse([g["L"] for g in groups], eye, bd_mask)
    sols = [_mdot(jnp.where(bd_mask, jnp.concatenate([t] * PACK, axis=0), 0.0), g["rhs"]).astype(MXU)
            for t, g in zip(tinvs, groups)]
    local = {}
    for g, sol in zip(groups, sols):
        for b, (c, j) in enumerate(g["members"]):
            r2 = _mdot(jnp.where(blk2 == b, g["lhs2"], 0.0), sol)
            aq = jnp.concatenate([r2[:DK, DV:], g["qeg"][b] - r2[DK:, DV:]], axis=0).astype(MXU)
            local[(g["reverse"], g["h"], j, c)] = (aq, r2[:DK, :DV], r2[DK:, :DV], g["egl"][b])
    nch = 2 * hb * rep
    S = [s_ref[ch] for ch in range(nch)]
    for step in range(nc):
        for ch in range(nch):
            reverse, hcol = divmod(ch, hb * rep)
            h, j = divmod(hcol, rep)
            c = nc - 1 - step if reverse else step
            aq, bm, op, egl = local[(reverse, h, j, c)]
            r = jnp.dot(aq, S[ch].astype(MXU), preferred_element_type=f32)
            (ob if reverse else of)[0, c * C:(c + 1) * C, hcol * DV:(hcol + 1) * DV] = r[DK:] + op
            S[ch] = S[ch] * egl - r[:DK] + bm
    for ch in range(nch):
        s_ref[ch] = S[ch]


def delta_rule(q, k, v, gt, tb=256, hb=4):
    B, T, _ = q.shape
    tb = min(tb, T)
    nT = T // tb
    rep = HV // HK
    fwd = lambda b, h, i: (b, i, h)
    bwd = lambda b, h, i: (b, nT - 1 - i, h)
    fwd0 = lambda b, h, i: (b, i, 0)
    bwd0 = lambda b, h, i: (b, nT - 1 - i, 0)
    qk_spec = lambda m: pl.BlockSpec((1, tb, hb * DK), m)
    v_spec = lambda m: pl.BlockSpec((1, tb, hb * rep * DV), m)
    g_spec = lambda m: pl.BlockSpec((1, tb, gt.shape[2]), m)
    o_shape = jax.ShapeDtypeStruct((B, T, V_DIM), f32)
    return pl.pallas_call(
        functools.partial(_delta_kernel_packed, hb=hb),
        grid=(B, HK // hb, nT),
        in_specs=[qk_spec(fwd), qk_spec(fwd), v_spec(fwd), g_spec(fwd0), qk_spec(bwd), qk_spec(bwd), v_spec(bwd), g_spec(bwd0)],
        out_specs=[v_spec(fwd), v_spec(bwd)],
        out_shape=[o_shape, o_shape],
        scratch_shapes=[pltpu.VMEM((2 * hb * rep, DK, DV), f32)],
        compiler_params=_params("parallel", "parallel", "arbitrary"),
        name="dn_delta",
    )(q, k, v, gt, q, k, v, gt)


def _gnorm_kernel(of_ref, ob_ref, z_ref, nw_ref, o_ref):
    o = of_ref[...] + ob_ref[...]
    z = z_ref[...]
    for h in range(o.shape[1] // DV):
        hs = slice(h * DV, (h + 1) * DV)
        seg = o[:, hs]
        y = seg * lax.rsqrt(jnp.mean(seg * seg, axis=-1, keepdims=True) + RMS_EPS) * nw_ref[...]
        o_ref[:, hs] = (y * _silu(z[:, hs])).astype(o_ref.dtype)


def gated_norm(of, ob, z, norm_w, tt=256, cb=1024):
    M, W = of.shape
    tt = min(tt, M)
    spec = pl.BlockSpec((tt, cb), lambda i, c: (i, c))
    return pl.pallas_call(
        _gnorm_kernel,
        grid=(M // tt, W // cb),
        in_specs=[spec, spec, spec, pl.BlockSpec((1, DV), lambda i, c: (0, 0))],
        out_specs=spec,
        out_shape=jax.ShapeDtypeStruct((M, W), MXU),
        compiler_params=_params("parallel", "parallel"),
        name="dn_gated_norm",
    )(of, ob, z, norm_w.reshape(1, DV).astype(f32))


def dn_layer(x, w_conv, w_z, w_ab, conv_w, a_log, dt_bias, norm_w, w_out, ln_g, ln_b):
    B, T, D = x.shape
    x2 = x.reshape(B * T, D)
    pc = matmul(x2, w_conv).reshape(B, T, CONV_DIM)
    z = matmul(x2, w_z)
    ab = matmul(x2, w_ab).reshape(B, T, -1)
    q, k, v = conv_qkv(pc, conv_w)
    gt = gates(ab, a_log, dt_bias)
    of, ob = delta_rule(q, k, v, gt)
    o = gated_norm(of.reshape(B * T, V_DIM), ob.reshape(B * T, V_DIM), z, norm_w)
    return matmul_res_ln(o, w_out, x2, ln_g, ln_b).reshape(B, T, D)


EXACT = jnp.bfloat16


def _split3(x):
    hi = x.astype(MXU)
    return hi, (x - hi.astype(f32)).astype(MXU)


def _router_kernel(x_ref, w_ref, atok_ref, a2_ref):
    xh, xl = _split3(x_ref[...])
    wh, wl = _split3(w_ref[...])
    dot = functools.partial(jnp.dot, preferred_element_type=f32)
    logits = dot(xh, wh) + (dot(xl, wh) + dot(xh, wl))
    lane = lax.broadcasted_iota(i32, logits.shape, 1)
    lm = jnp.where(lane < N_EXPERTS, logits, -jnp.inf)
    e = jnp.exp(lm - jnp.max(lm, axis=-1, keepdims=True))
    aff = e / jnp.sum(e, axis=-1, keepdims=True)
    atok_ref[...] = aff
    for j in range(aff.shape[0] // LANES):
        bt = aff[j * LANES:(j + 1) * LANES, :].T
        for ex in range(N_EXPERTS):
            a2_ref[j:j + 1, ex * LANES:(ex + 1) * LANES] = bt[ex:ex + 1, :]


def router(x2, w_router, tm=1024):
    n, D = x2.shape
    tm = min(tm, n)
    w = jnp.zeros((D, LANES), f32).at[:, :N_EXPERTS].set(w_router.astype(f32))
    return pl.pallas_call(
        _router_kernel,
        grid=(n // tm,),
        in_specs=[pl.BlockSpec((tm, D), lambda i: (i, 0)), pl.BlockSpec((D, LANES), lambda i: (0, 0))],
        out_specs=[pl.BlockSpec((tm, LANES), lambda i: (i, 0)), pl.BlockSpec((tm // LANES, N_EXPERTS * LANES), lambda i: (i, 0))],
        out_shape=[jax.ShapeDtypeStruct((n, LANES), f32), jax.ShapeDtypeStruct((n // LANES, N_EXPERTS * LANES), f32)],
        compiler_params=_params("parallel"),
        name="moe_router",
    )(x2, w)


def _per_expert(row, op):
    parts = []
    for ex in range(N_EXPERTS):
        s = op(row[:, ex * LANES:(ex + 1) * LANES], axis=1, keepdims=True)
        parts.append(jnp.broadcast_to(s, (1, LANES)))
    return jnp.concatenate(parts, axis=1)


SELECT_UNROLL = 8


def _select_kernel(a_ref, idx_ref, dst_ref, tab_ref, kmax_ref, *, cap, n):
    R = a_ref.shape[0]
    W = N_EXPERTS * LANES
    bits = pltpu.bitcast(a_ref[...], i32)
    capf = jnp.float32(cap)

    def count(mask):
        return _per_expert(jnp.sum(jnp.where(mask, 1.0, 0.0), axis=0, keepdims=True), jnp.sum)

    def bit_step(t, ans):
        cand = ans | lax.shift_left(jnp.int32(1), 29 - t)
        return jnp.where(count(bits >= cand) >= capf, cand, ans)

    thr = lax.fori_loop(0, 30, bit_step, jnp.zeros((1, W), i32))
    gt = bits > thr
    eq = bits == thr
    need = capf - count(gt)

    lane_r = lax.broadcasted_iota(i32, (LANES, LANES), 0)
    lane_c = lax.broadcasted_iota(i32, (LANES, LANES), 1)
    upper = jnp.where(lane_r <= lane_c, 1.0, 0.0).astype(EXACT)
    row_r = lax.broadcasted_iota(i32, (R, R), 0)
    row_c = lax.broadcasted_iota(i32, (R, R), 1)
    below = jnp.where(row_c < row_r, 1.0, 0.0).astype(EXACT)

    def prefix(slab):
        lc = jnp.dot(slab.astype(EXACT), upper, preferred_element_type=f32)
        tot = jnp.broadcast_to(lc[:, LANES - 1:LANES], (R, LANES))
        off = jnp.dot(below, tot.astype(EXACT), preferred_element_type=f32)
        return lc, off, tot

    tok = (lax.broadcasted_iota(i32, (R, LANES), 0) * LANES + lax.broadcasted_iota(i32, (R, LANES), 1))
    sub_r = lax.broadcasted_iota(i32, (R, LANES), 0).astype(f32)
    sub_l = lax.broadcasted_iota(i32, (LANES, LANES), 0).astype(f32)
    lane_p = lax.broadcasted_iota(i32, (1, LANES), 1)
    nsel = jnp.zeros((R, LANES), f32)
    cut_parts = []
    for ex in range(N_EXPERTS):
        es = slice(ex * LANES, (ex + 1) * LANES)
        eq_e = jnp.where(eq[:, es], 1.0, 0.0)
        lc, off, _ = prefix(eq_e)
        take = eq[:, es] & (lc + off <= need[:, es])
        cut_parts.append(jnp.broadcast_to(jnp.max(jnp.max(jnp.where(take, tok, -1), axis=0, keepdims=True), axis=1, keepdims=True), (1, LANES)))
        sel_e = jnp.where(gt[:, es] | take, 1.0, 0.0)
        lc, off, tot = prefix(sel_e)
        rank_e = nsel
        nsel = nsel + sel_e
        hi = jnp.floor(off * (1.0 / LANES))
        lo = off - hi * LANES
        table = jnp.concatenate([lc.T, rank_e.T, hi.T[0:8], lo.T[0:8]], axis=0).astype(EXACT)
        end = off + tot

        def chunk(c, carry):
            c = lax.convert_element_type(c, i32)
            p =(c * LANES + lane_p).astype(f32)
            jp = jnp.sum(jnp.where(end <= p, 1.0, 0.0), axis=0, keepdims=True)
            onehot = jnp.where(sub_r == jp, 1.0, 0.0).astype(EXACT)
            g = jnp.dot(table, onehot, preferred_element_type=f32)
            p_loc = p - (g[2 * LANES:2 * LANES + 1] * LANES + g[2 * LANES + 8:2 * LANES + 9])
            lp = jnp.sum(jnp.where(g[:LANES] <= p_loc, 1.0, 0.0), axis=0, keepdims=True)
            rk = jnp.sum(jnp.where(sub_l == lp, g[LANES:2 * LANES], 0.0), axis=0, keepdims=True)
            t = jp * LANES + lp
            col = pl.ds(pl.multiple_of(c * LANES, LANES), LANES)
            idx_ref[ex:ex + 1, col] = t.astype(i32)
            dst_ref[ex:ex + 1, col] = (rk * n + t).astype(i32)
            return carry

        lax.fori_loop(0, cap // LANES, chunk, 0, unroll=SELECT_UNROLL)

    diag = lax.broadcasted_iota(i32, (N_EXPERTS, LANES), 0) == lax.broadcasted_iota(i32, (N_EXPERTS, LANES), 1)
    thr_rows = jnp.concatenate([thr[:, ex * LANES:(ex + 1) * LANES] for ex in range(N_EXPERTS)], axis=0)
    cut_rows = jnp.concatenate(cut_parts, axis=0)
    tab_ref[...] = jnp.zeros_like(tab_ref)
    tab_ref[0:1, :] = jnp.sum(jnp.where(diag, thr_rows, 0), axis=0, keepdims=True)
    tab_ref[1:2, :] = jnp.sum(jnp.where(diag, cut_rows, 0), axis=0, keepdims=True)
    kmax_ref[...] = jnp.broadcast_to(jnp.max(nsel, axis=1, keepdims=True), (R, LANES)).astype(i32)


def select(a2, cap):
    R = a2.shape[0]
    n = R * LANES
    assert cap % LANES == 0
    return pl.pallas_call(
        functools.partial(_select_kernel, cap=cap, n=n),
        out_shape=[jax.ShapeDtypeStruct((N_EXPERTS, cap), i32), jax.ShapeDtypeStruct((N_EXPERTS, cap), i32),
                   jax.ShapeDtypeStruct((8, LANES), i32), jax.ShapeDtypeStruct((R, LANES), i32)],
        compiler_params=pltpu.CompilerParams(vmem_limit_bytes=V7X_VMEM_LIMIT),
        name="moe_select",
    )(a2)


ROW_DMA_UNROLL = 8

def _ffn_kernel(idx_ref, nidx_ref, dst_ref, x_hbm, wg_ref, wu_ref, wd_ref, y_hbm, xbuf, ybuf, gsem, ssem):
    tm = xbuf.shape[1]
    s = pl.program_id(0) * pl.num_programs(1) + pl.program_id(1)
    last = pl.num_programs(0) * pl.num_programs(1) - 1
    slot = s % 2

    def gather(ref, r, sl):
        return pltpu.make_async_copy(x_hbm.at[pl.ds(ref[0, 0, r], 1), :], xbuf.at[sl, pl.ds(r, 1), :], gsem.at[sl])

    def scatter(r, sl):
        return pltpu.make_async_copy(ybuf.at[sl, pl.ds(r, 1), :], y_hbm.at[pl.ds(dst_ref[0, 0, r], 1), :], ssem.at[sl])

    def each_row(fn):
        def body(r, c):
            fn(r)
            return c
        lax.fori_loop(0, tm, body, 0, unroll=ROW_DMA_UNROLL)

    @pl.when(s == 0)
    def _():
        each_row(lambda r: gather(idx_ref, r, 0).start())

    each_row(lambda r: gather(idx_ref, r, slot).wait())

    @pl.when(s >= 2)
    def _():
        each_row(lambda r: scatter(r, slot).wait())

    for r in range(tm):
        gather(nidx_ref, r, 1 - slot).start()
    xe = xbuf[slot].astype(MXU)
    gate = jnp.dot(xe, wg_ref[...], preferred_element_type=f32)
    up = jnp.dot(xe, wu_ref[...], preferred_element_type=f32)
    h = (_silu(gate) * up).astype(MXU)
    ybuf[slot] = jnp.dot(h, wd_ref[...], preferred_element_type=f32)
    for r in range(tm):
        scatter(r, slot).start()

    @pl.when(s == last)
    def _():
        each_row(lambda r: gather(nidx_ref, r, 1 - slot).wait())

        @pl.when(s >= 1)
        def _():
            each_row(lambda r: scatter(r, 1 - slot).wait())

        each_row(lambda r: scatter(r, slot).wait())


def moe_ffn(x2, idx, dst, w_gate, w_up, w_down, tm=256):
    n, D = x2.shape
    E, cap = idx.shape
    F = w_gate.shape[2]
    tm = min(tm, cap)
    nc = cap // tm
    ispec = pl.BlockSpec((1, 1, tm), lambda e, c: (e * nc + c, 0, 0), memory_space=pltpu.SMEM)
    nspec = pl.BlockSpec((1, 1, tm), lambda e, c: (jnp.minimum(e * nc + c + 1, E * nc - 1), 0, 0), memory_space=pltpu.SMEM)
    idx3 = idx.reshape(E * nc, 1, tm)
    return pl.pallas_call(
        _ffn_kernel,
        grid=(E, nc),
        in_specs=[
            ispec, nspec, ispec,
            pl.BlockSpec(memory_space=pl.ANY),
            pl.BlockSpec((None, D, F), lambda e, c: (e, 0, 0)),
            pl.BlockSpec((None, D, F), lambda e, c: (e, 0, 0)),
            pl.BlockSpec((None, F, D), lambda e, c: (e, 0, 0)),
        ],
        out_specs=pl.BlockSpec(memory_space=pl.ANY),
        out_shape=jax.ShapeDtypeStruct((E * n, D), f32),
        scratch_shapes=[pltpu.VMEM((2, tm, D), f32), pltpu.VMEM((2, tm, D), f32), pltpu.SemaphoreType.DMA((2,)),
                        pltpu.SemaphoreType.DMA((2,))],
        compiler_params=_params("arbitrary", "arbitrary"),
        name="moe_ffn",
    )(idx3, idx3, dst.reshape(E * nc, 1, tm), x2, w_gate, w_up, w_down)


COMBINE_COLS = 256


def _combine_kernel(kmax_ref, y_hbm, atok_ref, tab_ref, x_ref, g_ref, b_ref, o_ref, ybuf, gbuf, hbuf, sem):
    i = pl.program_id(0)
    nt = pl.num_programs(0)
    tt, D = x_ref.shape
    half = i % 2
    km = kmax_ref[i]

    def fetch(tile, k, hf):
        return pltpu.make_async_copy(y_hbm.at[pl.ds(k * (nt * tt) + tile * tt, tt), :], ybuf.at[hf, k], sem.at[hf, k])

    def each_slot(count, fn):
        def body(k, c):
            fn(k)
            return c
        lax.fori_loop(0, count, body, 0)

    @pl.when(i == 0)
    def _():
        each_slot(km, lambda k: fetch(0, k, 0).start())

    @pl.when(i + 1 < nt)
    def _():
        each_slot(kmax_ref[jnp.minimum(i + 1, nt - 1)], lambda k: fetch(i + 1, k, 1 - half).start())

    aff = atok_ref[...]
    bits = pltpu.bitcast(aff, i32)
    lane = lax.broadcasted_iota(i32, aff.shape, 1)
    tok = i * tt + lax.broadcasted_iota(i32, aff.shape, 0)
    thr, cut = tab_ref[0:1, :], tab_ref[1:2, :]
    sel = (lane < N_EXPERTS) & ((bits > thr) | ((bits == thr) & (tok <= cut)))
    before = jnp.where(lax.broadcasted_iota(i32, (LANES, LANES), 0) < lax.broadcasted_iota(i32, (LANES, LANES), 1), 1.0, 0.0)
    rank = jnp.dot(jnp.where(sel, 1.0, 0.0).astype(EXACT), before.astype(EXACT), preferred_element_type=f32)

    def slot_gate(k):
        mine = sel & (rank == lax.convert_element_type(k, f32))
        gk = jnp.sum(jnp.where(mine, aff, 0.0), axis=1, keepdims=True)
        has = jnp.sum(jnp.where(mine, 1.0, 0.0), axis=1, keepdims=True) > 0.0
        gbuf[k] = jnp.broadcast_to(jnp.where(has, gk, -1.0), (tt, LANES))

    each_slot(km, slot_gate)
    each_slot(km, lambda k: fetch(i, k, half).wait())
    reps = COMBINE_COLS // LANES
    for cc in range(D // COMBINE_COLS):
        cs = slice(cc * COMBINE_COLS, (cc + 1) * COMBINE_COLS)

        def add_slot(k, acc):
            gk = jnp.concatenate([gbuf[k]] * reps, axis=1)
            return acc + jnp.where(gk >= 0.0, ybuf[half, k, :, cs], 0.0) * gk

        hbuf[:, cs] = lax.fori_loop(0, km, add_slot, jnp.zeros((tt, COMBINE_COLS), f32))
    o_ref[...] = _layer_norm_rows(DEEPNORM_ALPHA * x_ref[...] + hbuf[...], g_ref[...], b_ref[...])


def moe_combine(y, atok, tab, kmax, x2, g, b, tt=LANES):
    n, D = x2.shape
    assert tt == LANES
    return pl.pallas_call(
        _combine_kernel,
        grid_spec=pltpu.PrefetchScalarGridSpec(
            num_scalar_prefetch=1,
            grid=(n // tt,),
            in_specs=[
                pl.BlockSpec(memory_space=pl.ANY),
                pl.BlockSpec((tt, LANES), lambda i, km: (i, 0)),
                pl.BlockSpec((8, LANES), lambda i, km: (0, 0)),
                pl.BlockSpec((tt, D), lambda i, km: (i, 0)),
                pl.BlockSpec((1, D), lambda i, km: (0, 0)),
                pl.BlockSpec((1, D), lambda i, km: (0, 0)),
            ],
            out_specs=pl.BlockSpec((tt, D), lambda i, km: (i, 0)),
            scratch_shapes=[pltpu.VMEM((2, N_EXPERTS, tt, D), f32), pltpu.VMEM((N_EXPERTS, tt, LANES), f32),
                            pltpu.VMEM((tt, D), f32), pltpu.SemaphoreType.DMA((2, N_EXPERTS))],
        ),
        out_shape=jax.ShapeDtypeStruct((n, D), f32),
        compiler_params=_params("arbitrary"),
        name="moe_combine",
    )(kmax, y, atok, tab, x2, g.reshape(1, D), b.reshape(1, D))


def moe_layer(x, w_router, w_gate, w_up, w_down, ln_g, ln_b):
    B, T, D = x.shape
    n = B * T
    x2 = x.reshape(n, D)
    cap = CAP_FACTOR * n // N_EXPERTS
    atok, a2 = router(x2, w_router)
    idx, dst, tab, kmax = select(a2, cap)
    y = moe_ffn(x2, idx, dst, w_gate, w_up, w_down)
    return moe_combine(y, atok, tab, kmax[:, 0], x2, ln_g, ln_b).reshape(B, T, D)


def _trunk(x, p):
    for i in range(DEPTH):
        j = i // 2
        if i % 2 == 0:
            x = attn_layer(x, p["attn_w_qkv"][j], p["attn_w_o"][j], p["attn_sink"][j], p["ln_mix_g"][i], p["ln_mix_b"][i])
        else:
            x = dn_layer(x, p["dn_w_conv"][j], p["dn_w_z"][j], p["dn_w_ab"][j], p["dn_conv_w"][j], p["dn_a_log"][j],
                         p["dn_dt_bias"][j], p["dn_norm_w"][j], p["dn_w_out"][j], p["ln_mix_g"][i], p["ln_mix_b"][i])
        x = moe_layer(x, p["moe_w_router"][i], p["moe_w_gate"][i], p["moe_w_up"][i], p["moe_w_down"][i],
                      p["ln_ffn_g"][i], p["ln_ffn_b"][i])
    return x


def kernel(x_prompt, x_sample, attn_w_qkv, attn_w_o, attn_sink, dn_w_in, dn_conv_w, dn_a_log, dn_dt_bias, dn_norm_w,
           dn_w_out, moe_w_router, moe_w_gate, moe_w_up, moe_w_down, ln_mix_g, ln_mix_b, ln_ffn_g, ln_ffn_b):
    c = lambda w: w.astype(MXU)
    p = dict(
        attn_w_qkv=c(attn_w_qkv), attn_w_o=c(attn_w_o), attn_sink=attn_sink,
        dn_w_conv=c(dn_w_in[:, :, :CONV_DIM]), dn_w_z=c(dn_w_in[:, :, CONV_DIM:CONV_DIM + V_DIM]),
        dn_w_ab=c(dn_w_in[:, :, CONV_DIM + V_DIM:]), dn_conv_w=dn_conv_w, dn_a_log=dn_a_log, dn_dt_bias=dn_dt_bias,
        dn_norm_w=dn_norm_w, dn_w_out=c(dn_w_out), moe_w_router=moe_w_router, moe_w_gate=c(moe_w_gate),
        moe_w_up=c(moe_w_up), moe_w_down=c(moe_w_down), ln_mix_g=ln_mix_g, ln_mix_b=ln_mix_b, ln_ffn_g=ln_ffn_g,
        ln_ffn_b=ln_ffn_b)
    return _trunk(x_prompt, p), _trunk(x_sample, p)
```

```python
import functools
import math

import jax
import jax.numpy as jnp
from jax import lax
from jax.experimental import pallas as pl
from jax.experimental.pallas import tpu as pltpu

f32, i32 = jnp.float32, jnp.int32
MXU = jnp.bfloat16

D_MODEL = 2048
DEPTH = 4
HQ, HKV, HD = 16, 4, 128
GROUP = HQ // HKV
BLOCK = 128
ROT_DIM = HD // 4
ROPE_THETA = 500000.0
HK, HV, DK, DV = 16, 32, 128, 128
QK_DIM, V_DIM = HK * DK, HV * DV
CONV_DIM = 2 * QK_DIM + V_DIM
CONV_K = 5
CHUNK = 64
N_EXPERTS = 16
D_EXPERT = 1024
CAP_FACTOR = 2
DEEPNORM_ALPHA = (2.0 * DEPTH) ** 0.25
LN_EPS = 1e-5
RMS_EPS = 1e-6

LANES = 128
V7X_VMEM_LIMIT = 56 * 1024 * 1024


def _params(*sem):
    return pltpu.CompilerParams(dimension_semantics=sem, vmem_limit_bytes=V7X_VMEM_LIMIT)


def _mm_kernel(a_ref, w_ref, o_ref):
    o_ref[...] = jnp.dot(a_ref[...].astype(MXU), w_ref[...], preferred_element_type=f32).astype(o_ref.dtype)


def matmul(a, w, layer, col0=0, ncols=None, out_dtype=f32, tm=1024, tn=1024):
    M, K = a.shape
    N = w.shape[2] - col0 if ncols is None else ncols
    tm, tn = min(tm, M), min(tn, N)
    assert M % tm == 0 and N % tn == 0 and col0 % tn == 0
    c0 = col0 // tn
    return pl.pallas_call(
        _mm_kernel,
        grid=(N // tn, M // tm),
        in_specs=[pl.BlockSpec((tm, K), lambda j, i: (i, 0)), pl.BlockSpec((None, K, tn), lambda j, i: (layer, 0, c0 + j))],
        out_specs=pl.BlockSpec((tm, tn), lambda j, i: (i, j)),
        out_shape=jax.ShapeDtypeStruct((M, N), out_dtype),
        compiler_params=_params("parallel", "parallel"),
        name="matmul",
    )(a, w)


def _layer_norm_rows(y, g, b):
    mu = jnp.mean(y, axis=-1, keepdims=True)
    yc = y - mu
    var = jnp.mean(yc * yc, axis=-1, keepdims=True)
    return yc * lax.rsqrt(var + LN_EPS) * g + b


def _mm_ln_kernel(a_ref, w_ref, x_ref, g_ref, b_ref, o_ref):
    h = jnp.dot(a_ref[...].astype(MXU), w_ref[...], preferred_element_type=f32)
    o_ref[...] = _layer_norm_rows(DEEPNORM_ALPHA * x_ref[...] + h, g_ref[...], b_ref[...])


def matmul_res_ln(a, w, layer, x, g, b, tm=256):
    M, K = a.shape
    D = w.shape[2]
    tm = min(tm, M)
    assert M % tm == 0
    return pl.pallas_call(
        _mm_ln_kernel,
        grid=(M // tm,),
        in_specs=[
            pl.BlockSpec((tm, K), lambda i: (i, 0)),
            pl.BlockSpec((None, K, D), lambda i: (layer, 0, 0), pipeline_mode=pl.Buffered(1)),
            pl.BlockSpec((tm, D), lambda i: (i, 0)),
            pl.BlockSpec((1, D), lambda i: (0, 0)),
            pl.BlockSpec((1, D), lambda i: (0, 0)),
        ],
        out_specs=pl.BlockSpec((tm, D), lambda i: (i, 0)),
        out_shape=jax.ShapeDtypeStruct((M, D), f32),
        compiler_params=_params("parallel"),
        name="matmul_res_ln",
    )(a, w, x, g.reshape(1, D), b.reshape(1, D))


def rope_table(T):
    half = ROT_DIM // 2
    inv = ROPE_THETA ** (-jnp.arange(0, ROT_DIM, 2, dtype=f32) / ROT_DIM)
    ang = jnp.arange(T, dtype=f32)[:, None] * inv[None, :]
    c, s = jnp.cos(ang), jnp.sin(ang)
    z = jnp.zeros((T, HD - ROT_DIM), f32)
    zh = jnp.zeros((T, half), f32)
    return jnp.concatenate([c, c, z + 1.0, -s, zh, z, zh, s, z], axis=1)


def _rope(t, tab):
    return t * tab[:, :HD] + pltpu.roll(t, HD - ROT_DIM // 2, 1) * tab[:, HD:2 * HD] + pltpu.roll(t, ROT_DIM // 2, 1) * tab[:, 2 * HD:]


def _attn_kernel(sink_ref, q_ref, kp_ref, kc_ref, kn_ref, vp_ref, vc_ref, vn_ref, tp_ref, tc_ref, tn_ref, o_ref):
    n = pl.program_id(1)
    nb = pl.num_programs(1)
    ii = lax.broadcasted_iota(i32, (GROUP * BLOCK, 3 * BLOCK), 0) % BLOCK
    jj = lax.broadcasted_iota(i32, (GROUP * BLOCK, 3 * BLOCK), 1)
    lo = jnp.where(n == 0, BLOCK, 0)
    hi = jnp.where(n == nb - 1, 2 * BLOCK, 3 * BLOCK)
    valid = (jj >= ii) & (jj <= ii + 2 * BLOCK) & (jj >= lo) & (jj < hi)
    tq = tc_ref[...]
    tabs = (tp_ref[...], tq, tn_ref[...])
    heads = range(HKV)
    hsl = [slice(h * HD, (h + 1) * HD) for h in heads]
    kbs = [jnp.concatenate([_rope(r[0, :, hs], t) for r, t in zip((kp_ref, kc_ref, kn_ref), tabs)], axis=0).astype(MXU)
           for hs in hsl]
    vbs = [jnp.concatenate([r[0, :, hs] for r in (vp_ref, vc_ref, vn_ref)], axis=0).astype(MXU) for hs in hsl]
    qhs = [jnp.concatenate([_rope(q_ref[0, :, (h * GROUP + g) * HD:(h * GROUP + g + 1) * HD], tq) for g in range(GROUP)],
                           axis=0).astype(MXU) for h in heads]
    ss = [jnp.where(valid, lax.dot_general(qh, kb, (((1,), (1,)), ((), ())), preferred_element_type=f32) * (HD ** -0.5),
                    -jnp.inf) for qh, kb in zip(qhs, kbs)]
    sks = [jnp.concatenate([jnp.full((BLOCK, 1), sink_ref[0, h * GROUP + g], f32) for g in range(GROUP)], axis=0)
           for h in heads]
    def lane_tiles(a):
        return [a[:, t * BLOCK:(t + 1) * BLOCK] for t in range(3)]

    ms = [jnp.maximum(jnp.max(functools.reduce(jnp.maximum, lane_tiles(s)), axis=-1, keepdims=True), sk)
          for s, sk in zip(ss, sks)]
    ps = [jnp.exp(s - m) for s, m in zip(ss, ms)]
    denoms = [jnp.sum(functools.reduce(jnp.add, lane_tiles(p)), axis=-1, keepdims=True) + jnp.exp(sk - m)
              for p, sk, m in zip(ps, sks, ms)]
    os_ = [jnp.dot((p / d).astype(MXU), vb, preferred_element_type=f32) for p, d, vb in zip(ps, denoms, vbs)]
    for h, o in zip(heads, os_):
        for g in range(GROUP):
            o_ref[0, :, (h * GROUP + g) * HD:(h * GROUP + g + 1) * HD] = o[g * BLOCK:(g + 1) * BLOCK].astype(o_ref.dtype)


def attention(qkv, sink):
    B, T, _ = qkv.shape
    out_dtype = MXU
    nb = T // BLOCK
    tab = rope_table(T)
    kcol, vcol = HQ // HKV, HQ // HKV + 1
    kvw = HKV * HD
    prev = lambda n: jnp.maximum(n - 1, 0)
    nxt = lambda n: jnp.minimum(n + 1, nb - 1)
    return pl.pallas_call(
        _attn_kernel,
        grid=(B, nb),
        in_specs=[
            pl.BlockSpec(memory_space=pltpu.SMEM),
            pl.BlockSpec((1, BLOCK, HQ * HD), lambda b, n: (b, n, 0)),
            pl.BlockSpec((1, BLOCK, kvw), lambda b, n: (b, prev(n), kcol)),
            pl.BlockSpec((1, BLOCK, kvw), lambda b, n: (b, n, kcol)),
            pl.BlockSpec((1, BLOCK, kvw), lambda b, n: (b, nxt(n), kcol)),
            pl.BlockSpec((1, BLOCK, kvw), lambda b, n: (b, prev(n), vcol)),
            pl.BlockSpec((1, BLOCK, kvw), lambda b, n: (b, n, vcol)),
            pl.BlockSpec((1, BLOCK, kvw), lambda b, n: (b, nxt(n), vcol)),
            pl.BlockSpec((BLOCK, 3 * HD), lambda b, n: (prev(n), 0)),
            pl.BlockSpec((BLOCK, 3 * HD), lambda b, n: (n, 0)),
            pl.BlockSpec((BLOCK, 3 * HD), lambda b, n: (nxt(n), 0)),
        ],
        out_specs=pl.BlockSpec((1, BLOCK, HQ * HD), lambda b, n: (b, n, 0)),
        out_shape=jax.ShapeDtypeStruct((B, T, HQ * HD), out_dtype),
        compiler_params=_params("parallel", "parallel"),
        name="attention",
    )(sink.reshape(1, HQ).astype(f32), qkv, qkv, qkv, qkv, qkv, qkv, qkv, tab, tab, tab)


def attn_layer(x, w_qkv, w_o, layer, sink, ln_g, ln_b):
    B, T, D = x.shape
    x2 = x.reshape(B * T, D)
    qkv = matmul(x2, w_qkv, layer).reshape(B, T, -1)
    o = attention(qkv, sink).reshape(B * T, HQ * HD)
    return matmul_res_ln(o, w_o, layer, x2, ln_g, ln_b).reshape(B, T, D)


def _silu(x):
    return x * jax.nn.sigmoid(x)


def _conv_kernel(xp_ref, xc_ref, xn_ref, w_ref, o_ref, *, l2, scale):
    i = pl.program_id(1)
    tt = xc_ref.shape[1]
    pad = CONV_K // 2
    xp = jnp.where(i == 0, 0.0, xp_ref[0])
    xn = jnp.where(i == pl.num_programs(1) - 1, 0.0, xn_ref[0])
    xe = jnp.concatenate([xp, xc_ref[0], xn], axis=0)
    acc = xe[8 - pad:8 - pad + tt] * w_ref[0:1, :]
    for j in range(1, CONV_K):
        acc = acc + xe[8 - pad + j:8 - pad + j + tt] * w_ref[j:j + 1, :]
    y = _silu(acc)
    for h in range(y.shape[1] // DK):
        seg = y[:, h * DK:(h + 1) * DK]
        if l2:
            seg = seg * lax.rsqrt(jnp.sum(seg * seg, axis=-1, keepdims=True) + RMS_EPS) * scale
        o_ref[0, :, h * DK:(h + 1) * DK] = seg


def _conv_part(proj, conv_w, col0, ncols, l2, scale, tt=512, cb=512):
    B, T, _ = proj.shape
    tt = min(tt, T)
    assert T % tt == 0 and col0 % cb == 0 and ncols % cb == 0
    c0 = col0 // cb
    r8 = tt // 8
    return pl.pallas_call(
        functools.partial(_conv_kernel, l2=l2, scale=scale),
        grid=(B, T // tt, ncols // cb),
        in_specs=[
            pl.BlockSpec((1, 8, cb), lambda b, i, c: (b, jnp.maximum(i * r8 - 1, 0), c0 + c)),
            pl.BlockSpec((1, tt, cb), lambda b, i, c: (b, i, c0 + c)),
            pl.BlockSpec((1, 8, cb), lambda b, i, c: (b, jnp.minimum((i + 1) * r8, T // 8 - 1), c0 + c)),
            pl.BlockSpec((CONV_K, cb), lambda b, i, c: (0, c0 + c)),
        ],
        out_specs=pl.BlockSpec((1, tt, cb), lambda b, i, c: (b, i, c)),
        out_shape=jax.ShapeDtypeStruct((B, T, ncols), f32),
        compiler_params=_params("parallel", "parallel", "parallel"),
        name="dn_conv",
    )(proj, proj, proj, conv_w)


def conv_qkv(proj, conv_w):
    q = _conv_part(proj, conv_w, 0, QK_DIM, True, DK ** -0.5)
    k = _conv_part(proj, conv_w, QK_DIM, QK_DIM, True, 1.0)
    v = _conv_part(proj, conv_w, 2 * QK_DIM, V_DIM, False, 1.0)
    return q, k, v


def _gates_kernel(ab_ref, alog_ref, dtb_ref, o_ref):
    x = ab_ref[0]
    tt = x.shape[0]
    g = -jnp.exp(alog_ref[...]) * jax.nn.softplus(x + dtb_ref[...])
    beta = jax.nn.sigmoid(x)
    ri = lax.broadcasted_iota(i32, (tt, tt), 0)
    ci = lax.broadcasted_iota(i32, (tt, tt), 1)
    same = (ri // CHUNK) == (ci // CHUNK)
    lp = jnp.where(same & (ci <= ri), 1.0, 0.0).astype(f32)
    ls = jnp.where(same & (ci >= ri), 1.0, 0.0).astype(f32)
    gp = jnp.dot(lp, g, precision=lax.Precision.HIGHEST, preferred_element_type=f32)
    gs = jnp.dot(ls, g, precision=lax.Precision.HIGHEST, preferred_element_type=f32)
    lane = lax.broadcasted_iota(i32, x.shape, 1)
    cum = jnp.where(lane < 2 * HV, gp, gs)
    o_ref[0] = jnp.where((lane % (2 * HV)) < HV, cum, beta)


def gates(ab, a_log, dt_bias, tt=256):
    B, T, W = ab.shape
    tt = min(tt, T)
    z = jnp.zeros((2, HV), f32)
    alog = jnp.concatenate([a_log.astype(f32), z], axis=1).reshape(1, W)
    dtb = jnp.concatenate([dt_bias.astype(f32), z], axis=1).reshape(1, W)
    return pl.pallas_call(
        _gates_kernel,
        grid=(B, T // tt),
        in_specs=[pl.BlockSpec((1, tt, W), lambda b, i: (b, i, 0)), pl.BlockSpec((1, W), lambda b, i: (0, 0)),
                  pl.BlockSpec((1, W), lambda b, i: (0, 0))],
        out_specs=pl.BlockSpec((1, tt, W), lambda b, i: (b, i, 0)),
        out_shape=jax.ShapeDtypeStruct((B, T, W), f32),
        compiler_params=_params("parallel", "parallel"),
        name="dn_gates",
    )(ab, alog, dtb)


PACK = 4


def _mdot(a, b):
    return jnp.dot(a.astype(MXU), b.astype(MXU), preferred_element_type=f32)


def _packed_unit_tri_inverse(Ls, eye, bd_mask):
    c = Ls[0].shape[0]

    def bd(m):
        return jnp.where(bd_mask, jnp.concatenate([m] * PACK, axis=0), 0.0)

    ms = [-L for L in Ls]
    ps = [jnp.where(eye, 1.0, 0.0) + m for m in ms]
    ms = [_mdot(m, bd(m)) for m in ms]
    for _ in range(int(math.log2(c)) - 2):
        rs = [_mdot(jnp.concatenate([m, p], axis=0), bd(m)) for m, p in zip(ms, ps)]
        ms, ps = [r[:c] for r in rs], [p + r[c:] for p, r in zip(ps, rs)]
    return [p + _mdot(p, bd(m)) for m, p in zip(ms, ps)]


def _delta_kernel_packed(qf, kf, vf, gf, qb, kb, vb, gb, of, ob, s_ref, *, hb):
    C = CHUNK
    rep = HV // HK
    W = PACK * C
    h0 = pl.program_id(1) * hb

    @pl.when(pl.program_id(2) == 0)
    def _():
        s_ref[...] = jnp.zeros_like(s_ref)

    tb = qf.shape[1]
    nc = tb // C
    row = lax.broadcasted_iota(i32, (C, W), 0)
    col = lax.broadcasted_iota(i32, (C, W), 1) % C
    blk = lax.broadcasted_iota(i32, (C, W), 1) // C
    eye = row == col
    bd_mask = (lax.broadcasted_iota(i32, (W, W), 0) // C) == (lax.broadcasted_iota(i32, (W, W), 1) // C)
    blk2 = lax.broadcasted_iota(i32, (DK + C, W), 1) // C
    def side_by_side(parts):
        out = jnp.broadcast_to(parts[-1], (C, W))
        for b in range(PACK - 2, -1, -1):
            out = jnp.where(blk == b, parts[b], out)
        return out

    groups = []
    for reverse, (q_ref, k_ref, v_ref, g_ref) in enumerate(((qf, kf, vf, gf), (qb, kb, vb, gb))):
        incl = (row <= col) if reverse else (row >= col)
        strict = (row < col) if reverse else (row > col)
        gt = g_ref[0]
        lane = lax.broadcasted_iota(i32, gt.shape, 1)
        for h in range(hb):
            lane_g = [(2 * HV if reverse else 0) + rep * (h0 + h) + j for j in range(rep)]
            gcols = [jnp.sum(jnp.where(lane == lg, gt, 0.0), axis=1, keepdims=True) for lg in lane_g]
            bcols = [jnp.sum(jnp.where(lane == lg + HV, gt, 0.0), axis=1, keepdims=True) for lg in lane_g]
            for c0 in range(0, nc, PACK // rep):
                members = [(c0 + ci, j) for ci in range(PACK // rep) for j in range(rep)]
                kcs = {c: k_ref[0, c * C:(c + 1) * C, h * DK:(h + 1) * DK] for c, _ in members}
                qcs = {c: q_ref[0, c * C:(c + 1) * C, h * DK:(h + 1) * DK] for c, _ in members}
                gc = [gcols[j][c * C:(c + 1) * C] for c, j in members]
                bc = [bcols[j][c * C:(c + 1) * C] for c, j in members]
                g4, b4 = side_by_side(gc), side_by_side(bc)
                gram = jnp.concatenate(
                    [lax.dot_general(jnp.concatenate([kcs[c], qcs[c]], axis=0).astype(MXU),
                                     jnp.concatenate([kcs[c]] * rep, axis=0).astype(MXU), (((1,), (1,)), ((), ())),
                                     preferred_element_type=f32) for c in sorted(kcs)], axis=1)
                kk, qk = gram[:C], gram[C:]
                kT = jnp.concatenate([kcs[c] for c, _ in members], axis=0).T
                grow = jnp.sum(jnp.where(eye, g4, 0.0), axis=0, keepdims=True)
                glast = g4[0:1] if reverse else g4[C - 1:C]
                decay = jnp.exp(jnp.where(incl, g4 - grow, -jnp.inf))
                rhs = jnp.concatenate(
                    [jnp.concatenate([v_ref[0, c * C:(c + 1) * C, (h * rep + j) * DV:(h * rep + j + 1) * DV] * bc[b],
                                      kcs[c] * (bc[b] * jnp.exp(gc[b]))], axis=1) for b, (c, j) in enumerate(members)],
                    axis=0)
                groups.append(dict(
                    reverse=reverse, h=h, members=members, L=jnp.where(strict, b4 * kk * decay, 0.0), rhs=rhs,
                    lhs2=jnp.concatenate([kT * jnp.exp(glast - grow), qk * decay], axis=0),
                    qeg=[qcs[c] * jnp.exp(gc[b]) for b, (c, j) in enumerate(members)],
                    egl=[jnp.exp(gc[b][0:1] if reverse else gc[b][C - 1:C]) for b in range(PACK)]))
    tinvs = _packed_unit_tri_inverse([g["L"] for g in groups], eye, bd_mask)
    sols = [_mdot(jnp.where(bd_mask, jnp.concatenate([t] * PACK, axis=0), 0.0), g["rhs"]).astype(MXU)
            for t, g in zip(tinvs, groups)]
    local = {}
    for g, sol in zip(groups, sols):
        for b, (c, j) in enumerate(g["members"]):
            r2 = _mdot(jnp.where(blk2 == b, g["lhs2"], 0.0), sol)
            aq = jnp.concatenate([r2[:DK, DV:], g["qeg"][b] - r2[DK:, DV:]], axis=0).astype(MXU)
            local[(g["reverse"], g["h"], j, c)] = (aq, r2[:DK, :DV], r2[DK:, :DV], g["egl"][b])
    nch = 2 * hb * rep
    S = [s_ref[ch] for ch in range(nch)]
    for step in range(nc):
        for ch in range(nch):
            reverse, hcol = divmod(ch, hb * rep)
            h, j = divmod(hcol, rep)
            c = nc - 1 - step if reverse else step
            aq, bm, op, egl = local[(reverse, h, j, c)]
            r = jnp.dot(aq, S[ch].astype(MXU), preferred_element_type=f32)
            (ob if reverse else of)[0, c * C:(c + 1) * C, hcol * DV:(hcol + 1) * DV] = r[DK:] + op
            S[ch] = S[ch] * egl - r[:DK] + bm
    for ch in range(nch):
        s_ref[ch] = S[ch]


def delta_rule(q, k, v, gt, tb=256, hb=4):
    B, T, _ = q.shape
    tb = min(tb, T)
    nT = T // tb
    rep = HV // HK
    fwd = lambda b, h, i: (b, i, h)
    bwd = lambda b, h, i: (b, nT - 1 - i, h)
    fwd0 = lambda b, h, i: (b, i, 0)
    bwd0 = lambda b, h, i: (b, nT - 1 - i, 0)
    qk_spec = lambda m: pl.BlockSpec((1, tb, hb * DK), m)
    v_spec = lambda m: pl.BlockSpec((1, tb, hb * rep * DV), m)
    g_spec = lambda m: pl.BlockSpec((1, tb, gt.shape[2]), m)
    o_shape = jax.ShapeDtypeStruct((B, T, V_DIM), f32)
    return pl.pallas_call(
        functools.partial(_delta_kernel_packed, hb=hb),
        grid=(B, HK // hb, nT),
        in_specs=[qk_spec(fwd), qk_spec(fwd), v_spec(fwd), g_spec(fwd0), qk_spec(bwd), qk_spec(bwd), v_spec(bwd), g_spec(bwd0)],
        out_specs=[v_spec(fwd), v_spec(bwd)],
        out_shape=[o_shape, o_shape],
        scratch_shapes=[pltpu.VMEM((2 * hb * rep, DK, DV), f32)],
        compiler_params=_params("parallel", "parallel", "arbitrary"),
        name="dn_delta",
    )(q, k, v, gt, q, k, v, gt)


def _gnorm_kernel(of_ref, ob_ref, z_ref, nw_ref, o_ref):
    o = of_ref[...] + ob_ref[...]
    z = z_ref[...]
    for h in range(o.shape[1] // DV):
        hs = slice(h * DV, (h + 1) * DV)
        seg = o[:, hs]
        y = seg * lax.rsqrt(jnp.mean(seg * seg, axis=-1, keepdims=True) + RMS_EPS) * nw_ref[...]
        o_ref[:, hs] = (y * _silu(z[:, hs])).astype(o_ref.dtype)


def gated_norm(of, ob, z, norm_w, tt=256, cb=1024):
    M, W = of.shape
    tt = min(tt, M)
    spec = pl.BlockSpec((tt, cb), lambda i, c: (i, c))
    return pl.pallas_call(
        _gnorm_kernel,
        grid=(M // tt, W // cb),
        in_specs=[spec, spec, spec, pl.BlockSpec((1, DV), lambda i, c: (0, 0))],
        out_specs=spec,
        out_shape=jax.ShapeDtypeStruct((M, W), MXU),
        compiler_params=_params("parallel", "parallel"),
        name="dn_gated_norm",
    )(of, ob, z, norm_w.reshape(1, DV).astype(f32))


def dn_layer(x, w_in, w_out, layer, conv_w, a_log, dt_bias, norm_w, ln_g, ln_b):
    B, T, D = x.shape
    x2 = x.reshape(B * T, D)
    pc = matmul(x2, w_in, layer, 0, CONV_DIM).reshape(B, T, CONV_DIM)
    z = matmul(x2, w_in, layer, CONV_DIM, V_DIM)
    ab = matmul(x2, w_in, layer, CONV_DIM + V_DIM, 4 * HV).reshape(B, T, -1)
    q, k, v = conv_qkv(pc, conv_w)
    gt = gates(ab, a_log, dt_bias)
    of, ob = delta_rule(q, k, v, gt)
    o = gated_norm(of.reshape(B * T, V_DIM), ob.reshape(B * T, V_DIM), z, norm_w)
    return matmul_res_ln(o, w_out, layer, x2, ln_g, ln_b).reshape(B, T, D)


EXACT = jnp.bfloat16


def _split3(x):
    hi = x.astype(MXU)
    return hi, (x - hi.astype(f32)).astype(MXU)


def _router_kernel(x_ref, w_ref, atok_ref, a2_ref):
    xh, xl = _split3(x_ref[...])
    wh, wl = _split3(w_ref[...])
    dot = functools.partial(jnp.dot, preferred_element_type=f32)
    logits = dot(xh, wh) + (dot(xl, wh) + dot(xh, wl))
    lane = lax.broadcasted_iota(i32, logits.shape, 1)
    lm = jnp.where(lane < N_EXPERTS, logits, -jnp.inf)
    e = jnp.exp(lm - jnp.max(lm, axis=-1, keepdims=True))
    aff = e / jnp.sum(e, axis=-1, keepdims=True)
    atok_ref[...] = aff
    for j in range(aff.shape[0] // LANES):
        bt = aff[j * LANES:(j + 1) * LANES, :].T
        for ex in range(N_EXPERTS):
            a2_ref[j:j + 1, ex * LANES:(ex + 1) * LANES] = bt[ex:ex + 1, :]


def router(x2, w_router, tm=1024):
    n, D = x2.shape
    tm = min(tm, n)
    w = jnp.zeros((D, LANES), f32).at[:, :N_EXPERTS].set(w_router.astype(f32))
    return pl.pallas_call(
        _router_kernel,
        grid=(n // tm,),
        in_specs=[pl.BlockSpec((tm, D), lambda i: (i, 0)), pl.BlockSpec((D, LANES), lambda i: (0, 0))],
        out_specs=[pl.BlockSpec((tm, LANES), lambda i: (i, 0)), pl.BlockSpec((tm // LANES, N_EXPERTS * LANES), lambda i: (i, 0))],
        out_shape=[jax.ShapeDtypeStruct((n, LANES), f32), jax.ShapeDtypeStruct((n // LANES, N_EXPERTS * LANES), f32)],
        compiler_params=_params("parallel"),
        name="moe_router",
    )(x2, w)


def _per_expert(row, op):
    parts = []
    for ex in range(N_EXPERTS):
        s = op(row[:, ex * LANES:(ex + 1) * LANES], axis=1, keepdims=True)
        parts.append(jnp.broadcast_to(s, (1, LANES)))
    return jnp.concatenate(parts, axis=1)


SELECT_UNROLL = 8


def _select_kernel(a_ref, idx_ref, dst_ref, tab_ref, kmax_ref, *, cap, n):
    R = a_ref.shape[0]
    W = N_EXPERTS * LANES
    bits = pltpu.bitcast(a_ref[...], i32)
    capf = jnp.float32(cap)

    def count(mask):
        return _per_expert(jnp.sum(jnp.where(mask, 1.0, 0.0), axis=0, keepdims=True), jnp.sum)

    def bit_step(t, ans):
        cand = ans | lax.shift_left(jnp.int32(1), 29 - t)
        return jnp.where(count(bits >= cand) >= capf, cand, ans)

    thr = lax.fori_loop(0, 30, bit_step, jnp.zeros((1, W), i32))
    gt = bits > thr
    eq = bits == thr
    need = capf - count(gt)

    lane_r = lax.broadcasted_iota(i32, (LANES, LANES), 0)
    lane_c = lax.broadcasted_iota(i32, (LANES, LANES), 1)
    upper = jnp.where(lane_r <= lane_c, 1.0, 0.0).astype(EXACT)
    row_r = lax.broadcasted_iota(i32, (R, R), 0)
    row_c = lax.broadcasted_iota(i32, (R, R), 1)
    below = jnp.where(row_c < row_r, 1.0, 0.0).astype(EXACT)

    def prefix(slab):
        lc = jnp.dot(slab.astype(EXACT), upper, preferred_element_type=f32)
        tot = jnp.broadcast_to(lc[:, LANES - 1:LANES], (R, LANES))
        off = jnp.dot(below, tot.astype(EXACT), preferred_element_type=f32)
        return lc, off, tot

    tok = (lax.broadcasted_iota(i32, (R, LANES), 0) * LANES + lax.broadcasted_iota(i32, (R, LANES), 1))
    sub_r = lax.broadcasted_iota(i32, (R, LANES), 0).astype(f32)
    sub_l = lax.broadcasted_iota(i32, (LANES, LANES), 0).astype(f32)
    lane_p = lax.broadcasted_iota(i32, (1, LANES), 1)
    nsel = jnp.zeros((R, LANES), f32)
    cut_parts = []
    for ex in range(N_EXPERTS):
        es = slice(ex * LANES, (ex + 1) * LANES)
        eq_e = jnp.where(eq[:, es], 1.0, 0.0)
        lc, off, _ = prefix(eq_e)
        take = eq[:, es] & (lc + off <= need[:, es])
        cut_parts.append(jnp.broadcast_to(jnp.max(jnp.max(jnp.where(take, tok, -1), axis=0, keepdims=True), axis=1, keepdims=True), (1, LANES)))
        sel_e = jnp.where(gt[:, es] | take, 1.0, 0.0)
        lc, off, tot = prefix(sel_e)
        rank_e = nsel
        nsel = nsel + sel_e
        hi = jnp.floor(off * (1.0 / LANES))
        lo = off - hi * LANES
        table = jnp.concatenate([lc.T, rank_e.T, hi.T[0:8], lo.T[0:8]], axis=0).astype(EXACT)
        end = off + tot

        def chunk(c, carry):
            c = lax.convert_element_type(c, i32)
            p =(c * LANES + lane_p).astype(f32)
            jp = jnp.sum(jnp.where(end <= p, 1.0, 0.0), axis=0, keepdims=True)
            onehot = jnp.where(sub_r == jp, 1.0, 0.0).astype(EXACT)
            g = jnp.dot(table, onehot, preferred_element_type=f32)
            p_loc = p - (g[2 * LANES:2 * LANES + 1] * LANES + g[2 * LANES + 8:2 * LANES + 9])
            lp = jnp.sum(jnp.where(g[:LANES] <= p_loc, 1.0, 0.0), axis=0, keepdims=True)
            rk = jnp.sum(jnp.where(sub_l == lp, g[LANES:2 * LANES], 0.0), axis=0, keepdims=True)
            t = jp * LANES + lp
            col = pl.ds(pl.multiple_of(c * LANES, LANES), LANES)
            idx_ref[ex:ex + 1, col] = t.astype(i32)
            dst_ref[ex:ex + 1, col] = (rk * n + t).astype(i32)
            return carry

        lax.fori_loop(0, cap // LANES, chunk, 0, unroll=SELECT_UNROLL)

    diag = lax.broadcasted_iota(i32, (N_EXPERTS, LANES), 0) == lax.broadcasted_iota(i32, (N_EXPERTS, LANES), 1)
    thr_rows = jnp.concatenate([thr[:, ex * LANES:(ex + 1) * LANES] for ex in range(N_EXPERTS)], axis=0)
    cut_rows = jnp.concatenate(cut_parts, axis=0)
    tab_ref[...] = jnp.zeros_like(tab_ref)
    tab_ref[0:1, :] = jnp.sum(jnp.where(diag, thr_rows, 0), axis=0, keepdims=True)
    tab_ref[1:2, :] = jnp.sum(jnp.where(diag, cut_rows, 0), axis=0, keepdims=True)
    kmax_ref[...] = jnp.broadcast_to(jnp.max(nsel, axis=1, keepdims=True), (R, LANES)).astype(i32)


def select(a2, cap):
    R = a2.shape[0]
    n = R * LANES
    assert cap % LANES == 0
    return pl.pallas_call(
        functools.partial(_select_kernel, cap=cap, n=n),
        out_shape=[jax.ShapeDtypeStruct((N_EXPERTS, cap), i32), jax.ShapeDtypeStruct((N_EXPERTS, cap), i32),
                   jax.ShapeDtypeStruct((8, LANES), i32), jax.ShapeDtypeStruct((R, LANES), i32)],
        compiler_params=pltpu.CompilerParams(vmem_limit_bytes=V7X_VMEM_LIMIT),
        name="moe_select",
    )(a2)


ROW_DMA_UNROLL = 8

def _ffn_kernel(idx_ref, nidx_ref, dst_ref, x_hbm, wg_ref, wu_ref, wd_ref, y_hbm, xbuf, ybuf, gsem, ssem):
    tm = xbuf.shape[1]
    s = pl.program_id(0) * pl.num_programs(1) + pl.program_id(1)
    last = pl.num_programs(0) * pl.num_programs(1) - 1
    slot = s % 2

    def gather(ref, r, sl):
        return pltpu.make_async_copy(x_hbm.at[pl.ds(ref[0, 0, r], 1), :], xbuf.at[sl, pl.ds(r, 1), :], gsem.at[sl])

    def scatter(r, sl):
        return pltpu.make_async_copy(ybuf.at[sl, pl.ds(r, 1), :], y_hbm.at[pl.ds(dst_ref[0, 0, r], 1), :], ssem.at[sl])

    def each_row(fn):
        def body(r, c):
            fn(r)
            return c
        lax.fori_loop(0, tm, body, 0, unroll=ROW_DMA_UNROLL)

    @pl.when(s == 0)
    def _():
        each_row(lambda r: gather(idx_ref, r, 0).start())

    each_row(lambda r: gather(idx_ref, r, slot).wait())

    @pl.when(s >= 2)
    def _():
        each_row(lambda r: scatter(r, slot).wait())

    for r in range(tm):
        gather(nidx_ref, r, 1 - slot).start()
    xe = xbuf[slot].astype(MXU)
    gate = jnp.dot(xe, wg_ref[...], preferred_element_type=f32)
    up = jnp.dot(xe, wu_ref[...], preferred_element_type=f32)
    h = (_silu(gate) * up).astype(MXU)
    ybuf[slot] = jnp.dot(h, wd_ref[...], preferred_element_type=f32)
    for r in range(tm):
        scatter(r, slot).start()

    @pl.when(s == last)
    def _():
        each_row(lambda r: gather(nidx_ref, r, 1 - slot).wait())

        @pl.when(s >= 1)
        def _():
            each_row(lambda r: scatter(r, 1 - slot).wait())

        each_row(lambda r: scatter(r, slot).wait())


def moe_ffn(x2, idx, dst, w_gate, w_up, w_down, layer, tm=256):
    n, D = x2.shape
    E, cap = idx.shape
    F = w_gate.shape[3]
    tm = min(tm, cap)
    nc = cap // tm
    ispec = pl.BlockSpec((1, 1, tm), lambda e, c: (e * nc + c, 0, 0), memory_space=pltpu.SMEM)
    nspec = pl.BlockSpec((1, 1, tm), lambda e, c: (jnp.minimum(e * nc + c + 1, E * nc - 1), 0, 0), memory_space=pltpu.SMEM)
    idx3 = idx.reshape(E * nc, 1, tm)
    return pl.pallas_call(
        _ffn_kernel,
        grid=(E, nc),
        in_specs=[
            ispec, nspec, ispec,
            pl.BlockSpec(memory_space=pl.ANY),
            pl.BlockSpec((None, None, D, F), lambda e, c: (layer, e, 0, 0)),
            pl.BlockSpec((None, None, D, F), lambda e, c: (layer, e, 0, 0)),
            pl.BlockSpec((None, None, F, D), lambda e, c: (layer, e, 0, 0)),
        ],
        out_specs=pl.BlockSpec(memory_space=pl.ANY),
        out_shape=jax.ShapeDtypeStruct((E * n, D), f32),
        scratch_shapes=[pltpu.VMEM((2, tm, D), f32), pltpu.VMEM((2, tm, D), f32), pltpu.SemaphoreType.DMA((2,)),
                        pltpu.SemaphoreType.DMA((2,))],
        compiler_params=_params("arbitrary", "arbitrary"),
        name="moe_ffn",
    )(idx3, idx3, dst.reshape(E * nc, 1, tm), x2, w_gate, w_up, w_down)


COMBINE_COLS = 256


def _combine_kernel(kmax_ref, y_hbm, atok_ref, tab_ref, x_ref, g_ref, b_ref, o_ref, ybuf, gbuf, hbuf, sem):
    i = pl.program_id(0)
    nt = pl.num_programs(0)
    tt, D = x_ref.shape
    half = i % 2
    km = kmax_ref[i]

    def fetch(tile, k, hf):
        return pltpu.make_async_copy(y_hbm.at[pl.ds(k * (nt * tt) + tile * tt, tt), :], ybuf.at[hf, k], sem.at[hf, k])

    def each_slot(count, fn):
        def body(k, c):
            fn(k)
            return c
        lax.fori_loop(0, count, body, 0)

    @pl.when(i == 0)
    def _():
        each_slot(km, lambda k: fetch(0, k, 0).start())

    @pl.when(i + 1 < nt)
    def _():
        each_slot(kmax_ref[jnp.minimum(i + 1, nt - 1)], lambda k: fetch(i + 1, k, 1 - half).start())

    aff = atok_ref[...]
    bits = pltpu.bitcast(aff, i32)
    lane = lax.broadcasted_iota(i32, aff.shape, 1)
    tok = i * tt + lax.broadcasted_iota(i32, aff.shape, 0)
    thr, cut = tab_ref[0:1, :], tab_ref[1:2, :]
    sel = (lane < N_EXPERTS) & ((bits > thr) | ((bits == thr) & (tok <= cut)))
    before = jnp.where(lax.broadcasted_iota(i32, (LANES, LANES), 0) < lax.broadcasted_iota(i32, (LANES, LANES), 1), 1.0, 0.0)
    rank = jnp.dot(jnp.where(sel, 1.0, 0.0).astype(EXACT), before.astype(EXACT), preferred_element_type=f32)

    def slot_gate(k):
        mine = sel & (rank == lax.convert_element_type(k, f32))
        gk = jnp.sum(jnp.where(mine, aff, 0.0), axis=1, keepdims=True)
        has = jnp.sum(jnp.where(mine, 1.0, 0.0), axis=1, keepdims=True) > 0.0
        gbuf[k] = jnp.broadcast_to(jnp.where(has, gk, -1.0), (tt, LANES))

    each_slot(km, slot_gate)
    each_slot(km, lambda k: fetch(i, k, half).wait())
    reps = COMBINE_COLS // LANES
    for cc in range(D // COMBINE_COLS):
        cs = slice(cc * COMBINE_COLS, (cc + 1) * COMBINE_COLS)

        def add_slot(k, acc):
            gk = jnp.concatenate([gbuf[k]] * reps, axis=1)
            return acc + jnp.where(gk >= 0.0, ybuf[half, k, :, cs], 0.0) * gk

        hbuf[:, cs] = lax.fori_loop(0, km, add_slot, jnp.zeros((tt, COMBINE_COLS), f32))
    o_ref[...] = _layer_norm_rows(DEEPNORM_ALPHA * x_ref[...] + hbuf[...], g_ref[...], b_ref[...])


def moe_combine(y, atok, tab, kmax, x2, g, b, tt=LANES):
    n, D = x2.shape
    assert tt == LANES
    return pl.pallas_call(
        _combine_kernel,
        grid_spec=pltpu.PrefetchScalarGridSpec(
            num_scalar_prefetch=1,
            grid=(n // tt,),
            in_specs=[
                pl.BlockSpec(memory_space=pl.ANY),
                pl.BlockSpec((tt, LANES), lambda i, km: (i, 0)),
                pl.BlockSpec((8, LANES), lambda i, km: (0, 0)),
                pl.BlockSpec((tt, D), lambda i, km: (i, 0)),
                pl.BlockSpec((1, D), lambda i, km: (0, 0)),
                pl.BlockSpec((1, D), lambda i, km: (0, 0)),
            ],
            out_specs=pl.BlockSpec((tt, D), lambda i, km: (i, 0)),
            scratch_shapes=[pltpu.VMEM((2, N_EXPERTS, tt, D), f32), pltpu.VMEM((N_EXPERTS, tt, LANES), f32),
                            pltpu.VMEM((tt, D), f32), pltpu.SemaphoreType.DMA((2, N_EXPERTS))],
        ),
        out_shape=jax.ShapeDtypeStruct((n, D), f32),
        compiler_params=_params("arbitrary"),
        name="moe_combine",
    )(kmax, y, atok, tab, x2, g.reshape(1, D), b.reshape(1, D))


def moe_layer(x, w_router, w_gate, w_up, w_down, layer, ln_g, ln_b):
    B, T, D = x.shape
    n = B * T
    x2 = x.reshape(n, D)
    cap = CAP_FACTOR * n // N_EXPERTS
    atok, a2 = router(x2, w_router)
    idx, dst, tab, kmax = select(a2, cap)
    y = moe_ffn(x2, idx, dst, w_gate, w_up, w_down, layer)
    return moe_combine(y, atok, tab, kmax[:, 0], x2, ln_g, ln_b).reshape(B, T, D)


def _trunk(x, p):
    for i in range(DEPTH):
        j = i // 2
        if i % 2 == 0:
            x = attn_layer(x, p["attn_w_qkv"], p["attn_w_o"], j, p["attn_sink"][j], p["ln_mix_g"][i], p["ln_mix_b"][i])
        else:
            x = dn_layer(x, p["dn_w_in"], p["dn_w_out"], j, p["dn_conv_w"][j], p["dn_a_log"][j], p["dn_dt_bias"][j],
                         p["dn_norm_w"][j], p["ln_mix_g"][i], p["ln_mix_b"][i])
        x = moe_layer(x, p["moe_w_router"][i], p["moe_w_gate"], p["moe_w_up"], p["moe_w_down"], i,
                      p["ln_ffn_g"][i], p["ln_ffn_b"][i])
    return x


def kernel(x_prompt, x_sample, attn_w_qkv, attn_w_o, attn_sink, dn_w_in, dn_conv_w, dn_a_log, dn_dt_bias, dn_norm_w,
           dn_w_out, moe_w_router, moe_w_gate, moe_w_up, moe_w_down, ln_mix_g, ln_mix_b, ln_ffn_g, ln_ffn_b):
    c = lambda w: w.astype(MXU)
    p = dict(
        attn_w_qkv=c(attn_w_qkv), attn_w_o=c(attn_w_o), attn_sink=attn_sink,
        dn_w_in=c(dn_w_in), dn_conv_w=dn_conv_w, dn_a_log=dn_a_log, dn_dt_bias=dn_dt_bias,
        dn_norm_w=dn_norm_w, dn_w_out=c(dn_w_out), moe_w_router=moe_w_router, moe_w_gate=c(moe_w_gate),
        moe_w_up=c(moe_w_up), moe_w_down=c(moe_w_down), ln_mix_g=ln_mix_g, ln_mix_b=ln_mix_b, ln_ffn_g=ln_ffn_g,
        ln_ffn_b=ln_ffn_b)
    return _trunk(x_prompt, p), _trunk(x_sample, p)
```

```python
import functools
import math

import jax
import jax.numpy as jnp
from jax import lax
from jax.experimental import pallas as pl
from jax.experimental.pallas import tpu as pltpu

f32, i32 = jnp.float32, jnp.int32
MXU = jnp.bfloat16

D_MODEL = 2048
DEPTH = 4
HQ, HKV, HD = 16, 4, 128
GROUP = HQ // HKV
BLOCK = 128
ROT_DIM = HD // 4
ROPE_THETA = 500000.0
HK, HV, DK, DV = 16, 32, 128, 128
QK_DIM, V_DIM = HK * DK, HV * DV
CONV_DIM = 2 * QK_DIM + V_DIM
CONV_K = 5
CHUNK = 64
N_EXPERTS = 16
D_EXPERT = 1024
CAP_FACTOR = 2
DEEPNORM_ALPHA = (2.0 * DEPTH) ** 0.25
LN_EPS = 1e-5
RMS_EPS = 1e-6

LANES = 128
V7X_VMEM_LIMIT = 56 * 1024 * 1024


def _params(*sem):
    return pltpu.CompilerParams(dimension_semantics=sem, vmem_limit_bytes=V7X_VMEM_LIMIT)


def _mm_kernel(a_ref, w_ref, o_ref):
    o_ref[...] = jnp.dot(a_ref[...].astype(MXU), w_ref[...], preferred_element_type=f32).astype(o_ref.dtype)


def matmul(a, w, layer, col0=0, ncols=None, out_dtype=f32, tm=1024, tn=1024):
    M, K = a.shape
    N = w.shape[2] - col0 if ncols is None else ncols
    tm, tn = min(tm, M), min(tn, N)
    assert M % tm == 0 and N % tn == 0 and col0 % tn == 0
    c0 = col0 // tn
    return pl.pallas_call(
        _mm_kernel,
        grid=(N // tn, M // tm),
        in_specs=[pl.BlockSpec((tm, K), lambda j, i: (i, 0)), pl.BlockSpec((None, K, tn), lambda j, i: (layer, 0, c0 + j))],
        out_specs=pl.BlockSpec((tm, tn), lambda j, i: (i, j)),
        out_shape=jax.ShapeDtypeStruct((M, N), out_dtype),
        compiler_params=_params("parallel", "parallel"),
        name="matmul",
    )(a, w)


def _layer_norm_rows(y, g, b):
    mu = jnp.mean(y, axis=-1, keepdims=True)
    yc = y - mu
    var = jnp.mean(yc * yc, axis=-1, keepdims=True)
    return yc * lax.rsqrt(var + LN_EPS) * g + b


def _mm_ln_kernel(a_ref, w_ref, x_ref, g_ref, b_ref, o_ref):
    h = jnp.dot(a_ref[...].astype(MXU), w_ref[...], preferred_element_type=f32)
    o_ref[...] = _layer_norm_rows(DEEPNORM_ALPHA * x_ref[...] + h, g_ref[...], b_ref[...])


def matmul_res_ln(a, w, layer, x, g, b, tm=256):
    M, K = a.shape
    D = w.shape[2]
    tm = min(tm, M)
    assert M % tm == 0
    return pl.pallas_call(
        _mm_ln_kernel,
        grid=(M // tm,),
        in_specs=[
            pl.BlockSpec((tm, K), lambda i: (i, 0)),
            pl.BlockSpec((None, K, D), lambda i: (layer, 0, 0), pipeline_mode=pl.Buffered(1)),
            pl.BlockSpec((tm, D), lambda i: (i, 0)),
            pl.BlockSpec((1, D), lambda i: (0, 0)),
            pl.BlockSpec((1, D), lambda i: (0, 0)),
        ],
        out_specs=pl.BlockSpec((tm, D), lambda i: (i, 0)),
        out_shape=jax.ShapeDtypeStruct((M, D), f32),
        compiler_params=_params("parallel"),
        name="matmul_res_ln",
    )(a, w, x, g.reshape(1, D), b.reshape(1, D))


def rope_table(T):
    half = ROT_DIM // 2
    inv = ROPE_THETA ** (-jnp.arange(0, ROT_DIM, 2, dtype=f32) / ROT_DIM)
    ang = jnp.arange(T, dtype=f32)[:, None] * inv[None, :]
    c, s = jnp.cos(ang), jnp.sin(ang)
    z = jnp.zeros((T, HD - ROT_DIM), f32)
    zh = jnp.zeros((T, half), f32)
    return jnp.concatenate([c, c, z + 1.0, -s, zh, z, zh, s, z], axis=1)


def _rope(t, tab):
    return t * tab[:, :HD] + pltpu.roll(t, HD - ROT_DIM // 2, 1) * tab[:, HD:2 * HD] + pltpu.roll(t, ROT_DIM // 2, 1) * tab[:, 2 * HD:]


def _attn_kernel(sink_ref, q_ref, kp_ref, kc_ref, kn_ref, vp_ref, vc_ref, vn_ref, tp_ref, tc_ref, tn_ref, o_ref):
    n = pl.program_id(1)
    nb = pl.num_programs(1)
    ii = lax.broadcasted_iota(i32, (GROUP * BLOCK, 3 * BLOCK), 0) % BLOCK
    jj = lax.broadcasted_iota(i32, (GROUP * BLOCK, 3 * BLOCK), 1)
    lo = jnp.where(n == 0, BLOCK, 0)
    hi = jnp.where(n == nb - 1, 2 * BLOCK, 3 * BLOCK)
    valid = (jj >= ii) & (jj <= ii + 2 * BLOCK) & (jj >= lo) & (jj < hi)
    tq = tc_ref[...]
    tabs = (tp_ref[...], tq, tn_ref[...])
    heads = range(HKV)
    hsl = [slice(h * HD, (h + 1) * HD) for h in heads]
    kbs = [jnp.concatenate([_rope(r[0, :, hs], t) for r, t in zip((kp_ref, kc_ref, kn_ref), tabs)], axis=0).astype(MXU)
           for hs in hsl]
    vbs = [jnp.concatenate([r[0, :, hs] for r in (vp_ref, vc_ref, vn_ref)], axis=0).astype(MXU) for hs in hsl]
    qhs = [jnp.concatenate([_rope(q_ref[0, :, (h * GROUP + g) * HD:(h * GROUP + g + 1) * HD], tq) for g in range(GROUP)],
                           axis=0).astype(MXU) for h in heads]
    ss = [jnp.where(valid, lax.dot_general(qh, kb, (((1,), (1,)), ((), ())), preferred_element_type=f32) * (HD ** -0.5),
                    -jnp.inf) for qh, kb in zip(qhs, kbs)]
    sks = [jnp.concatenate([jnp.full((BLOCK, 1), sink_ref[0, h * GROUP + g], f32) for g in range(GROUP)], axis=0)
           for h in heads]
    def lane_tiles(a):
        return [a[:, t * BLOCK:(t + 1) * BLOCK] for t in range(3)]

    ms = [jnp.maximum(jnp.max(functools.reduce(jnp.maximum, lane_tiles(s)), axis=-1, keepdims=True), sk)
          for s, sk in zip(ss, sks)]
    ps = [jnp.exp(s - m) for s, m in zip(ss, ms)]
    denoms = [jnp.sum(functools.reduce(jnp.add, lane_tiles(p)), axis=-1, keepdims=True) + jnp.exp(sk - m)
              for p, sk, m in zip(ps, sks, ms)]
    os_ = [jnp.dot((p / d).astype(MXU), vb, preferred_element_type=f32) for p, d, vb in zip(ps, denoms, vbs)]
    for h, o in zip(heads, os_):
        for g in range(GROUP):
            o_ref[0, :, (h * GROUP + g) * HD:(h * GROUP + g + 1) * HD] = o[g * BLOCK:(g + 1) * BLOCK].astype(o_ref.dtype)


def attention(qkv, sink):
    B, T, _ = qkv.shape
    out_dtype = MXU
    nb = T // BLOCK
    tab = rope_table(T)
    kcol, vcol = HQ // HKV, HQ // HKV + 1
    kvw = HKV * HD
    prev = lambda n: jnp.maximum(n - 1, 0)
    nxt = lambda n: jnp.minimum(n + 1, nb - 1)
    return pl.pallas_call(
        _attn_kernel,
        grid=(B, nb),
        in_specs=[
            pl.BlockSpec(memory_space=pltpu.SMEM),
            pl.BlockSpec((1, BLOCK, HQ * HD), lambda b, n: (b, n, 0)),
            pl.BlockSpec((1, BLOCK, kvw), lambda b, n: (b, prev(n), kcol)),
            pl.BlockSpec((1, BLOCK, kvw), lambda b, n: (b, n, kcol)),
            pl.BlockSpec((1, BLOCK, kvw), lambda b, n: (b, nxt(n), kcol)),
            pl.BlockSpec((1, BLOCK, kvw), lambda b, n: (b, prev(n), vcol)),
            pl.BlockSpec((1, BLOCK, kvw), lambda b, n: (b, n, vcol)),
            pl.BlockSpec((1, BLOCK, kvw), lambda b, n: (b, nxt(n), vcol)),
            pl.BlockSpec((BLOCK, 3 * HD), lambda b, n: (prev(n), 0)),
            pl.BlockSpec((BLOCK, 3 * HD), lambda b, n: (n, 0)),
            pl.BlockSpec((BLOCK, 3 * HD), lambda b, n: (nxt(n), 0)),
        ],
        out_specs=pl.BlockSpec((1, BLOCK, HQ * HD), lambda b, n: (b, n, 0)),
        out_shape=jax.ShapeDtypeStruct((B, T, HQ * HD), out_dtype),
        compiler_params=_params("parallel", "parallel"),
        name="attention",
    )(sink.reshape(1, HQ).astype(f32), qkv, qkv, qkv, qkv, qkv, qkv, qkv, tab, tab, tab)


def attn_layer(x, w_qkv, w_o, layer, sink, ln_g, ln_b):
    B, T, D = x.shape
    x2 = x.reshape(B * T, D)
    qkv = matmul(x2, w_qkv, layer).reshape(B, T, -1)
    o = attention(qkv, sink).reshape(B * T, HQ * HD)
    return matmul_res_ln(o, w_o, layer, x2, ln_g, ln_b).reshape(B, T, D)


def _silu(x):
    return x * jax.nn.sigmoid(x)


def _conv_kernel(xp_ref, xc_ref, xn_ref, w_ref, o_ref, *, l2, scale):
    i = pl.program_id(1)
    tt = xc_ref.shape[1]
    pad = CONV_K // 2
    xp = jnp.where(i == 0, 0.0, xp_ref[0])
    xn = jnp.where(i == pl.num_programs(1) - 1, 0.0, xn_ref[0])
    xe = jnp.concatenate([xp, xc_ref[0], xn], axis=0)
    acc = xe[8 - pad:8 - pad + tt] * w_ref[0:1, :]
    for j in range(1, CONV_K):
        acc = acc + xe[8 - pad + j:8 - pad + j + tt] * w_ref[j:j + 1, :]
    y = _silu(acc)
    for h in range(y.shape[1] // DK):
        seg = y[:, h * DK:(h + 1) * DK]
        if l2:
            seg = seg * lax.rsqrt(jnp.sum(seg * seg, axis=-1, keepdims=True) + RMS_EPS) * scale
        o_ref[0, :, h * DK:(h + 1) * DK] = seg


def _conv_part(proj, conv_w, col0, ncols, l2, scale, tt=512, cb=512):
    B, T, _ = proj.shape
    tt = min(tt, T)
    assert T % tt == 0 and col0 % cb == 0 and ncols % cb == 0
    c0 = col0 // cb
    r8 = tt // 8
    return pl.pallas_call(
        functools.partial(_conv_kernel, l2=l2, scale=scale),
        grid=(B, T // tt, ncols // cb),
        in_specs=[
            pl.BlockSpec((1, 8, cb), lambda b, i, c: (b, jnp.maximum(i * r8 - 1, 0), c0 + c)),
            pl.BlockSpec((1, tt, cb), lambda b, i, c: (b, i, c0 + c)),
            pl.BlockSpec((1, 8, cb), lambda b, i, c: (b, jnp.minimum((i + 1) * r8, T // 8 - 1), c0 + c)),
            pl.BlockSpec((CONV_K, cb), lambda b, i, c: (0, c0 + c)),
        ],
        out_specs=pl.BlockSpec((1, tt, cb), lambda b, i, c: (b, i, c)),
        out_shape=jax.ShapeDtypeStruct((B, T, ncols), f32),
        compiler_params=_params("parallel", "parallel", "parallel"),
        name="dn_conv",
    )(proj, proj, proj, conv_w)


def conv_qkv(proj, conv_w):
    q = _conv_part(proj, conv_w, 0, QK_DIM, True, DK ** -0.5)
    k = _conv_part(proj, conv_w, QK_DIM, QK_DIM, True, 1.0)
    v = _conv_part(proj, conv_w, 2 * QK_DIM, V_DIM, False, 1.0)
    return q, k, v


def _gates_kernel(ab_ref, alog_ref, dtb_ref, o_ref):
    x = ab_ref[0]
    tt = x.shape[0]
    g = -jnp.exp(alog_ref[...]) * jax.nn.softplus(x + dtb_ref[...])
    beta = jax.nn.sigmoid(x)
    ri = lax.broadcasted_iota(i32, (tt, tt), 0)
    ci = lax.broadcasted_iota(i32, (tt, tt), 1)
    same = (ri // CHUNK) == (ci // CHUNK)
    lp = jnp.where(same & (ci <= ri), 1.0, 0.0).astype(f32)
    ls = jnp.where(same & (ci >= ri), 1.0, 0.0).astype(f32)
    gp = jnp.dot(lp, g, precision=lax.Precision.HIGHEST, preferred_element_type=f32)
    gs = jnp.dot(ls, g, precision=lax.Precision.HIGHEST, preferred_element_type=f32)
    lane = lax.broadcasted_iota(i32, x.shape, 1)
    cum = jnp.where(lane < 2 * HV, gp, gs)
    o_ref[0] = jnp.where((lane % (2 * HV)) < HV, cum, beta)


def gates(ab, a_log, dt_bias, tt=256):
    B, T, W = ab.shape
    tt = min(tt, T)
    z = jnp.zeros((2, HV), f32)
    alog = jnp.concatenate([a_log.astype(f32), z], axis=1).reshape(1, W)
    dtb = jnp.concatenate([dt_bias.astype(f32), z], axis=1).reshape(1, W)
    return pl.pallas_call(
        _gates_kernel,
        grid=(B, T // tt),
        in_specs=[pl.BlockSpec((1, tt, W), lambda b, i: (b, i, 0)), pl.BlockSpec((1, W), lambda b, i: (0, 0)),
                  pl.BlockSpec((1, W), lambda b, i: (0, 0))],
        out_specs=pl.BlockSpec((1, tt, W), lambda b, i: (b, i, 0)),
        out_shape=jax.ShapeDtypeStruct((B, T, W), f32),
        compiler_params=_params("parallel", "parallel"),
        name="dn_gates",
    )(ab, alog, dtb)


PACK = 4


def _mdot(a, b):
    return jnp.dot(a.astype(MXU), b.astype(MXU), preferred_element_type=f32)


def _packed_unit_tri_inverse(Ls, eye, bd_mask):
    c = Ls[0].shape[0]

    def bd(m):
        return jnp.where(bd_mask, jnp.concatenate([m] * PACK, axis=0), 0.0)

    ms = [-L for L in Ls]
    ps = [jnp.where(eye, 1.0, 0.0) + m for m in ms]
    ms = [_mdot(m, bd(m)) for m in ms]
    for _ in range(int(math.log2(c)) - 2):
        rs = [_mdot(jnp.concatenate([m, p], axis=0), bd(m)) for m, p in zip(ms, ps)]
        ms, ps = [r[:c] for r in rs], [p + r[c:] for p, r in zip(ps, rs)]
    return [p + _mdot(p, bd(m)) for m, p in zip(ms, ps)]


def _delta_kernel_packed(qf, kf, vf, gf, qb, kb, vb, gb, of, ob, s_ref, *, hb):
    C = CHUNK
    rep = HV // HK
    W = PACK * C
    h0 = pl.program_id(1) * hb

    @pl.when(pl.program_id(2) == 0)
    def _():
        s_ref[...] = jnp.zeros_like(s_ref)

    tb = qf.shape[1]
    nc = tb // C
    row = lax.broadcasted_iota(i32, (C, W), 0)
    col = lax.broadcasted_iota(i32, (C, W), 1) % C
    blk = lax.broadcasted_iota(i32, (C, W), 1) // C
    eye = row == col
    bd_mask = (lax.broadcasted_iota(i32, (W, W), 0) // C) == (lax.broadcasted_iota(i32, (W, W), 1) // C)
    blk2 = lax.broadcasted_iota(i32, (DK + C, W), 1) // C
    def side_by_side(parts):
        out = jnp.broadcast_to(parts[-1], (C, W))
        for b in range(PACK - 2, -1, -1):
            out = jnp.where(blk == b, parts[b], out)
        return out

    groups = []
    for reverse, (q_ref, k_ref, v_ref, g_ref) in enumerate(((qf, kf, vf, gf), (qb, kb, vb, gb))):
        incl = (row <= col) if reverse else (row >= col)
        strict = (row < col) if reverse else (row > col)
        gt = g_ref[0]
        lane = lax.broadcasted_iota(i32, gt.shape, 1)
        for h in range(hb):
            lane_g = [(2 * HV if reverse else 0) + rep * (h0 + h) + j for j in range(rep)]
            gcols = [jnp.sum(jnp.where(lane == lg, gt, 0.0), axis=1, keepdims=True) for lg in lane_g]
            bcols = [jnp.sum(jnp.where(lane == lg + HV, gt, 0.0), axis=1, keepdims=True) for lg in lane_g]
            for c0 in range(0, nc, PACK // rep):
                members = [(c0 + ci, j) for ci in range(PACK // rep) for j in range(rep)]
                kcs = {c: k_ref[0, c * C:(c + 1) * C, h * DK:(h + 1) * DK] for c, _ in members}
                qcs = {c: q_ref[0, c * C:(c + 1) * C, h * DK:(h + 1) * DK] for c, _ in members}
                gc = [gcols[j][c * C:(c + 1) * C] for c, j in members]
                bc = [bcols[j][c * C:(c + 1) * C] for c, j in members]
                g4, b4 = side_by_side(gc), side_by_side(bc)
                gram = jnp.concatenate(
                    [lax.dot_general(jnp.concatenate([kcs[c], qcs[c]], axis=0).astype(MXU),
                                     jnp.concatenate([kcs[c]] * rep, axis=0).astype(MXU), (((1,), (1,)), ((), ())),
                                     preferred_element_type=f32) for c in sorted(kcs)], axis=1)
                kk, qk = gram[:C], gram[C:]
                kT = jnp.concatenate([kcs[c] for c, _ in members], axis=0).T
                grow = jnp.sum(jnp.where(eye, g4, 0.0), axis=0, keepdims=True)
                glast = g4[0:1] if reverse else g4[C - 1:C]
                decay = jnp.exp(jnp.where(incl, g4 - grow, -jnp.inf))
                rhs = jnp.concatenate(
                    [jnp.concatenate([v_ref[0, c * C:(c + 1) * C, (h * rep + j) * DV:(h * rep + j + 1) * DV] * bc[b],
                                      kcs[c] * (bc[b] * jnp.exp(gc[b]))], axis=1) for b, (c, j) in enumerate(members)],
                    axis=0)
                groups.append(dict(
                    reverse=reverse, h=h, members=members, L=jnp.where(strict, b4 * kk * decay, 0.0), rhs=rhs,
                    lhs2=jnp.concatenate([kT * jnp.exp(glast - grow), qk * decay], axis=0),
                    qeg=[qcs[c] * jnp.exp(gc[b]) for b, (c, j) in enumerate(members)],
                    egl=[jnp.exp(gc[b][0:1] if reverse else gc[b][C - 1:C]) for b in range(PACK)]))
    tinvs = _packed_unit_tri_inverse([g["L"] for g in groups], eye, bd_mask)
    sols = [_mdot(jnp.where(bd_mask, jnp.concatenate([t] * PACK, axis=0), 0.0), g["rhs"]).astype(MXU)
            for t, g in zip(tinvs, groups)]
    local = {}
    for g, sol in zip(groups, sols):
        for b, (c, j) in enumerate(g["members"]):
            r2 = _mdot(jnp.where(blk2 == b, g["lhs2"], 0.0), sol)
            aq = jnp.concatenate([r2[:DK, DV:], g["qeg"][b] - r2[DK:, DV:]], axis=0).astype(MXU)
            local[(g["reverse"], g["h"], j, c)] = (aq, r2[:DK, :DV], r2[DK:, :DV], g["egl"][b])
    nch = 2 * hb * rep
    S = [s_ref[ch] for ch in range(nch)]
    for step in range(nc):
        for ch in range(nch):
            reverse, hcol = divmod(ch, hb * rep)
            h, j = divmod(hcol, rep)
            c = nc - 1 - step if reverse else step
            aq, bm, op, egl = local[(reverse, h, j, c)]
            r = jnp.dot(aq, S[ch].astype(MXU), preferred_element_type=f32)
            (ob if reverse else of)[0, c * C:(c + 1) * C, hcol * DV:(hcol + 1) * DV] = r[DK:] + op
            S[ch] = S[ch] * egl - r[:DK] + bm
    for ch in range(nch):
        s_ref[ch] = S[ch]


def delta_rule(q, k, v, gt, tb=256, hb=4):
    B, T, _ = q.shape
    tb = min(tb, T)
    nT = T // tb
    rep = HV // HK
    fwd = lambda b, h, i: (b, i, h)
    bwd = lambda b, h, i: (b, nT - 1 - i, h)
    fwd0 = lambda b, h, i: (b, i, 0)
    bwd0 = lambda b, h, i: (b, nT - 1 - i, 0)
    qk_spec = lambda m: pl.BlockSpec((1, tb, hb * DK), m)
    v_spec = lambda m: pl.BlockSpec((1, tb, hb * rep * DV), m)
    g_spec = lambda m: pl.BlockSpec((1, tb, gt.shape[2]), m)
    o_shape = jax.ShapeDtypeStruct((B, T, V_DIM), f32)
    return pl.pallas_call(
        functools.partial(_delta_kernel_packed, hb=hb),
        grid=(B, HK // hb, nT),
        in_specs=[qk_spec(fwd), qk_spec(fwd), v_spec(fwd), g_spec(fwd0), qk_spec(bwd), qk_spec(bwd), v_spec(bwd), g_spec(bwd0)],
        out_specs=[v_spec(fwd), v_spec(bwd)],
        out_shape=[o_shape, o_shape],
        scratch_shapes=[pltpu.VMEM((2 * hb * rep, DK, DV), f32)],
        compiler_params=_params("parallel", "parallel", "arbitrary"),
        name="dn_delta",
    )(q, k, v, gt, q, k, v, gt)


def _gnorm_mm_ln_kernel(of_ref, ob_ref, z_ref, nw_ref, w_ref, x_ref, g_ref, b_ref, o_ref, a_ref):
    for h in range(of_ref.shape[1] // DV):
        hs = slice(h * DV, (h + 1) * DV)
        seg = of_ref[:, hs] + ob_ref[:, hs]
        y = seg * lax.rsqrt(jnp.mean(seg * seg, axis=-1, keepdims=True) + RMS_EPS) * nw_ref[...]
        a_ref[:, hs] = (y * _silu(z_ref[:, hs])).astype(a_ref.dtype)
    h = jnp.dot(a_ref[...], w_ref[...], preferred_element_type=f32)
    o_ref[...] = _layer_norm_rows(DEEPNORM_ALPHA * x_ref[...] + h, g_ref[...], b_ref[...])


def gated_norm_matmul_res_ln(of, ob, z, norm_w, w, layer, x, g, b, tm=128):
    M, K = of.shape
    D = w.shape[2]
    tm = min(tm, M)
    assert M % tm == 0
    aspec = pl.BlockSpec((tm, K), lambda i: (i, 0))
    vec = pl.BlockSpec((1, D), lambda i: (0, 0))
    return pl.pallas_call(
        _gnorm_mm_ln_kernel,
        grid=(M // tm,),
        in_specs=[aspec, aspec, aspec, pl.BlockSpec((1, DV), lambda i: (0, 0)),
                  pl.BlockSpec((None, K, D), lambda i: (layer, 0, 0), pipeline_mode=pl.Buffered(1)),
                  pl.BlockSpec((tm, D), lambda i: (i, 0)), vec, vec],
        out_specs=pl.BlockSpec((tm, D), lambda i: (i, 0)),
        out_shape=jax.ShapeDtypeStruct((M, D), f32),
        scratch_shapes=[pltpu.VMEM((tm, K), MXU)],
        compiler_params=_params("parallel"),
        name="gnorm_matmul_res_ln",
    )(of, ob, z, norm_w.reshape(1, DV).astype(f32), w, x, g.reshape(1, D), b.reshape(1, D))


def dn_layer(x, w_in, w_out, layer, conv_w, a_log, dt_bias, norm_w, ln_g, ln_b):
    B, T, D = x.shape
    x2 = x.reshape(B * T, D)
    pc = matmul(x2, w_in, layer, 0, CONV_DIM).reshape(B, T, CONV_DIM)
    z = matmul(x2, w_in, layer, CONV_DIM, V_DIM)
    ab = matmul(x2, w_in, layer, CONV_DIM + V_DIM, 4 * HV).reshape(B, T, -1)
    q, k, v = conv_qkv(pc, conv_w)
    gt = gates(ab, a_log, dt_bias)
    of, ob = delta_rule(q, k, v, gt)
    return gated_norm_matmul_res_ln(of.reshape(B * T, V_DIM), ob.reshape(B * T, V_DIM), z, norm_w, w_out, layer, x2,
                                    ln_g, ln_b).reshape(B, T, D)


EXACT = jnp.bfloat16


def _split3(x):
    hi = x.astype(MXU)
    return hi, (x - hi.astype(f32)).astype(MXU)


def _router_kernel(x_ref, w_ref, atok_ref, a2_ref):
    xh, xl = _split3(x_ref[...])
    wh, wl = _split3(w_ref[...])
    dot = functools.partial(jnp.dot, preferred_element_type=f32)
    logits = dot(xh, wh) + (dot(xl, wh) + dot(xh, wl))
    lane = lax.broadcasted_iota(i32, logits.shape, 1)
    lm = jnp.where(lane < N_EXPERTS, logits, -jnp.inf)
    e = jnp.exp(lm - jnp.max(lm, axis=-1, keepdims=True))
    aff = e / jnp.sum(e, axis=-1, keepdims=True)
    atok_ref[...] = aff
    for j in range(aff.shape[0] // LANES):
        bt = aff[j * LANES:(j + 1) * LANES, :].T
        for ex in range(N_EXPERTS):
            a2_ref[j:j + 1, ex * LANES:(ex + 1) * LANES] = bt[ex:ex + 1, :]


def router(x2, w_router, tm=1024):
    n, D = x2.shape
    tm = min(tm, n)
    w = jnp.zeros((D, LANES), f32).at[:, :N_EXPERTS].set(w_router.astype(f32))
    return pl.pallas_call(
        _router_kernel,
        grid=(n // tm,),
        in_specs=[pl.BlockSpec((tm, D), lambda i: (i, 0)), pl.BlockSpec((D, LANES), lambda i: (0, 0))],
        out_specs=[pl.BlockSpec((tm, LANES), lambda i: (i, 0)), pl.BlockSpec((tm // LANES, N_EXPERTS * LANES), lambda i: (i, 0))],
        out_shape=[jax.ShapeDtypeStruct((n, LANES), f32), jax.ShapeDtypeStruct((n // LANES, N_EXPERTS * LANES), f32)],
        compiler_params=_params("parallel"),
        name="moe_router",
    )(x2, w)


def _per_expert(row, op):
    parts = []
    for ex in range(N_EXPERTS):
        s = op(row[:, ex * LANES:(ex + 1) * LANES], axis=1, keepdims=True)
        parts.append(jnp.broadcast_to(s, (1, LANES)))
    return jnp.concatenate(parts, axis=1)


SELECT_UNROLL = 8


def _select_kernel(a_ref, idx_ref, dst_ref, tab_ref, kmax_ref, *, cap, n):
    R = a_ref.shape[0]
    W = N_EXPERTS * LANES
    bits = pltpu.bitcast(a_ref[...], i32)
    capf = jnp.float32(cap)

    def count(mask):
        return _per_expert(jnp.sum(jnp.where(mask, 1.0, 0.0), axis=0, keepdims=True), jnp.sum)

    def bit_step(t, ans):
        cand = ans | lax.shift_left(jnp.int32(1), 29 - t)
        return jnp.where(count(bits >= cand) >= capf, cand, ans)

    thr = lax.fori_loop(0, 30, bit_step, jnp.zeros((1, W), i32))
    gt = bits > thr
    eq = bits == thr
    need = capf - count(gt)

    lane_r = lax.broadcasted_iota(i32, (LANES, LANES), 0)
    lane_c = lax.broadcasted_iota(i32, (LANES, LANES), 1)
    upper = jnp.where(lane_r <= lane_c, 1.0, 0.0).astype(EXACT)
    row_r = lax.broadcasted_iota(i32, (R, R), 0)
    row_c = lax.broadcasted_iota(i32, (R, R), 1)
    below = jnp.where(row_c < row_r, 1.0, 0.0).astype(EXACT)

    def prefix(slab):
        lc = jnp.dot(slab.astype(EXACT), upper, preferred_element_type=f32)
        tot = jnp.broadcast_to(lc[:, LANES - 1:LANES], (R, LANES))
        off = jnp.dot(below, tot.astype(EXACT), preferred_element_type=f32)
        return lc, off, tot

    tok = (lax.broadcasted_iota(i32, (R, LANES), 0) * LANES + lax.broadcasted_iota(i32, (R, LANES), 1))
    sub_r = lax.broadcasted_iota(i32, (R, LANES), 0).astype(f32)
    sub_l = lax.broadcasted_iota(i32, (LANES, LANES), 0).astype(f32)
    lane_p = lax.broadcasted_iota(i32, (1, LANES), 1)
    nsel = jnp.zeros((R, LANES), f32)
    cut_parts = []
    for ex in range(N_EXPERTS):
        es = slice(ex * LANES, (ex + 1) * LANES)
        eq_e = jnp.where(eq[:, es], 1.0, 0.0)
        lc, off, _ = prefix(eq_e)
        take = eq[:, es] & (lc + off <= need[:, es])
        cut_parts.append(jnp.broadcast_to(jnp.max(jnp.max(jnp.where(take, tok, -1), axis=0, keepdims=True), axis=1, keepdims=True), (1, LANES)))
        sel_e = jnp.where(gt[:, es] | take, 1.0, 0.0)
        lc, off, tot = prefix(sel_e)
        rank_e = nsel
        nsel = nsel + sel_e
        hi = jnp.floor(off * (1.0 / LANES))
        lo = off - hi * LANES
        table = jnp.concatenate([lc.T, rank_e.T, hi.T[0:8], lo.T[0:8]], axis=0).astype(EXACT)
        end = off + tot

        def chunk(c, carry):
            c = lax.convert_element_type(c, i32)
            p =(c * LANES + lane_p).astype(f32)
            jp = jnp.sum(jnp.where(end <= p, 1.0, 0.0), axis=0, keepdims=True)
            onehot = jnp.where(sub_r == jp, 1.0, 0.0).astype(EXACT)
            g = jnp.dot(table, onehot, preferred_element_type=f32)
            p_loc = p - (g[2 * LANES:2 * LANES + 1] * LANES + g[2 * LANES + 8:2 * LANES + 9])
            lp = jnp.sum(jnp.where(g[:LANES] <= p_loc, 1.0, 0.0), axis=0, keepdims=True)
            rk = jnp.sum(jnp.where(sub_l == lp, g[LANES:2 * LANES], 0.0), axis=0, keepdims=True)
            t = jp * LANES + lp
            col = pl.ds(pl.multiple_of(c * LANES, LANES), LANES)
            idx_ref[ex:ex + 1, col] = t.astype(i32)
            dst_ref[ex:ex + 1, col] = (rk * n + t).astype(i32)
            return carry

        lax.fori_loop(0, cap // LANES, chunk, 0, unroll=SELECT_UNROLL)

    diag = lax.broadcasted_iota(i32, (N_EXPERTS, LANES), 0) == lax.broadcasted_iota(i32, (N_EXPERTS, LANES), 1)
    thr_rows = jnp.concatenate([thr[:, ex * LANES:(ex + 1) * LANES] for ex in range(N_EXPERTS)], axis=0)
    cut_rows = jnp.concatenate(cut_parts, axis=0)
    tab_ref[...] = jnp.zeros_like(tab_ref)
    tab_ref[0:1, :] = jnp.sum(jnp.where(diag, thr_rows, 0), axis=0, keepdims=True)
    tab_ref[1:2, :] = jnp.sum(jnp.where(diag, cut_rows, 0), axis=0, keepdims=True)
    kmax_ref[...] = jnp.broadcast_to(jnp.max(nsel, axis=1, keepdims=True), (R, LANES)).astype(i32)


def select(a2, cap):
    R = a2.shape[0]
    n = R * LANES
    assert cap % LANES == 0
    return pl.pallas_call(
        functools.partial(_select_kernel, cap=cap, n=n),
        out_shape=[jax.ShapeDtypeStruct((N_EXPERTS, cap), i32), jax.ShapeDtypeStruct((N_EXPERTS, cap), i32),
                   jax.ShapeDtypeStruct((8, LANES), i32), jax.ShapeDtypeStruct((R, LANES), i32)],
        compiler_params=pltpu.CompilerParams(vmem_limit_bytes=V7X_VMEM_LIMIT),
        name="moe_select",
    )(a2)


ROW_DMA_UNROLL = 8

def _ffn_kernel(idx_ref, nidx_ref, dst_ref, x_hbm, wg_ref, wu_ref, wd_ref, y_hbm, xbuf, ybuf, gsem, ssem):
    tm = xbuf.shape[1]
    s = pl.program_id(0) * pl.num_programs(1) + pl.program_id(1)
    last = pl.num_programs(0) * pl.num_programs(1) - 1
    slot = s % 2

    def gather(ref, r, sl):
        return pltpu.make_async_copy(x_hbm.at[pl.ds(ref[0, 0, r], 1), :], xbuf.at[sl, pl.ds(r, 1), :], gsem.at[sl])

    def scatter(r, sl):
        return pltpu.make_async_copy(ybuf.at[sl, pl.ds(r, 1), :], y_hbm.at[pl.ds(dst_ref[0, 0, r], 1), :], ssem.at[sl])

    def each_row(fn):
        def body(r, c):
            fn(r)
            return c
        lax.fori_loop(0, tm, body, 0, unroll=ROW_DMA_UNROLL)

    @pl.when(s == 0)
    def _():
        each_row(lambda r: gather(idx_ref, r, 0).start())

    each_row(lambda r: gather(idx_ref, r, slot).wait())

    @pl.when(s >= 2)
    def _():
        each_row(lambda r: scatter(r, slot).wait())

    for r in range(tm):
        gather(nidx_ref, r, 1 - slot).start()
    xe = xbuf[slot].astype(MXU)
    gate = jnp.dot(xe, wg_ref[...], preferred_element_type=f32)
    up = jnp.dot(xe, wu_ref[...], preferred_element_type=f32)
    h = (_silu(gate) * up).astype(MXU)
    ybuf[slot] = jnp.dot(h, wd_ref[...], preferred_element_type=f32)
    for r in range(tm):
        scatter(r, slot).start()

    @pl.when(s == last)
    def _():
        each_row(lambda r: gather(nidx_ref, r, 1 - slot).wait())

        @pl.when(s >= 1)
        def _():
            each_row(lambda r: scatter(r, 1 - slot).wait())

        each_row(lambda r: scatter(r, slot).wait())


def moe_ffn(x2, idx, dst, w_gate, w_up, w_down, layer, tm=256):
    n, D = x2.shape
    E, cap = idx.shape
    F = w_gate.shape[3]
    tm = min(tm, cap)
    nc = cap // tm
    ispec = pl.BlockSpec((1, 1, tm), lambda e, c: (e * nc + c, 0, 0), memory_space=pltpu.SMEM)
    nspec = pl.BlockSpec((1, 1, tm), lambda e, c: (jnp.minimum(e * nc + c + 1, E * nc - 1), 0, 0), memory_space=pltpu.SMEM)
    idx3 = idx.reshape(E * nc, 1, tm)
    return pl.pallas_call(
        _ffn_kernel,
        grid=(E, nc),
        in_specs=[
            ispec, nspec, ispec,
            pl.BlockSpec(memory_space=pl.ANY),
            pl.BlockSpec((None, None, D, F), lambda e, c: (layer, e, 0, 0)),
            pl.BlockSpec((None, None, D, F), lambda e, c: (layer, e, 0, 0)),
            pl.BlockSpec((None, None, F, D), lambda e, c: (layer, e, 0, 0)),
        ],
        out_specs=pl.BlockSpec(memory_space=pl.ANY),
        out_shape=jax.ShapeDtypeStruct((E * n, D), f32),
        scratch_shapes=[pltpu.VMEM((2, tm, D), f32), pltpu.VMEM((2, tm, D), f32), pltpu.SemaphoreType.DMA((2,)),
                        pltpu.SemaphoreType.DMA((2,))],
        compiler_params=_params("arbitrary", "arbitrary"),
        name="moe_ffn",
    )(idx3, idx3, dst.reshape(E * nc, 1, tm), x2, w_gate, w_up, w_down)


COMBINE_COLS = 256


def _combine_kernel(kmax_ref, y_hbm, atok_ref, tab_ref, x_ref, g_ref, b_ref, o_ref, ybuf, gbuf, hbuf, sem):
    i = pl.program_id(0)
    nt = pl.num_programs(0)
    tt, D = x_ref.shape
    half = i % 2
    km = kmax_ref[i]

    def fetch(tile, k, hf):
        return pltpu.make_async_copy(y_hbm.at[pl.ds(k * (nt * tt) + tile * tt, tt), :], ybuf.at[hf, k], sem.at[hf, k])

    def each_slot(count, fn):
        def body(k, c):
            fn(k)
            return c
        lax.fori_loop(0, count, body, 0)

    @pl.when(i == 0)
    def _():
        each_slot(km, lambda k: fetch(0, k, 0).start())

    @pl.when(i + 1 < nt)
    def _():
        each_slot(kmax_ref[jnp.minimum(i + 1, nt - 1)], lambda k: fetch(i + 1, k, 1 - half).start())

    aff = atok_ref[...]
    bits = pltpu.bitcast(aff, i32)
    lane = lax.broadcasted_iota(i32, aff.shape, 1)
    tok = i * tt + lax.broadcasted_iota(i32, aff.shape, 0)
    thr, cut = tab_ref[0:1, :], tab_ref[1:2, :]
    sel = (lane < N_EXPERTS) & ((bits > thr) | ((bits == thr) & (tok <= cut)))
    before = jnp.where(lax.broadcasted_iota(i32, (LANES, LANES), 0) < lax.broadcasted_iota(i32, (LANES, LANES), 1), 1.0, 0.0)
    rank = jnp.dot(jnp.where(sel, 1.0, 0.0).astype(EXACT), before.astype(EXACT), preferred_element_type=f32)

    def slot_gate(k):
        mine = sel & (rank == lax.convert_element_type(k, f32))
        gk = jnp.sum(jnp.where(mine, aff, 0.0), axis=1, keepdims=True)
        has = jnp.sum(jnp.where(mine, 1.0, 0.0), axis=1, keepdims=True) > 0.0
        gbuf[k] = jnp.broadcast_to(jnp.where(has, gk, -1.0), (tt, LANES))

    each_slot(km, slot_gate)
    each_slot(km, lambda k: fetch(i, k, half).wait())
    reps = COMBINE_COLS // LANES
    for cc in range(D // COMBINE_COLS):
        cs = slice(cc * COMBINE_COLS, (cc + 1) * COMBINE_COLS)

        def add_slot(k, acc):
            gk = jnp.concatenate([gbuf[k]] * reps, axis=1)
            return acc + jnp.where(gk >= 0.0, ybuf[half, k, :, cs], 0.0) * gk

        hbuf[:, cs] = lax.fori_loop(0, km, add_slot, jnp.zeros((tt, COMBINE_COLS), f32))
    o_ref[...] = _layer_norm_rows(DEEPNORM_ALPHA * x_ref[...] + hbuf[...], g_ref[...], b_ref[...])


def moe_combine(y, atok, tab, kmax, x2, g, b, tt=LANES):
    n, D = x2.shape
    assert tt == LANES
    return pl.pallas_call(
        _combine_kernel,
        grid_spec=pltpu.PrefetchScalarGridSpec(
            num_scalar_prefetch=1,
            grid=(n // tt,),
            in_specs=[
                pl.BlockSpec(memory_space=pl.ANY),
                pl.BlockSpec((tt, LANES), lambda i, km: (i, 0)),
                pl.BlockSpec((8, LANES), lambda i, km: (0, 0)),
                pl.BlockSpec((tt, D), lambda i, km: (i, 0)),
                pl.BlockSpec((1, D), lambda i, km: (0, 0)),
                pl.BlockSpec((1, D), lambda i, km: (0, 0)),
            ],
            out_specs=pl.BlockSpec((tt, D), lambda i, km: (i, 0)),
            scratch_shapes=[pltpu.VMEM((2, N_EXPERTS, tt, D), f32), pltpu.VMEM((N_EXPERTS, tt, LANES), f32),
                            pltpu.VMEM((tt, D), f32), pltpu.SemaphoreType.DMA((2, N_EXPERTS))],
        ),
        out_shape=jax.ShapeDtypeStruct((n, D), f32),
        compiler_params=_params("arbitrary"),
        name="moe_combine",
    )(kmax, y, atok, tab, x2, g.reshape(1, D), b.reshape(1, D))


def moe_layer(x, w_router, w_gate, w_up, w_down, layer, ln_g, ln_b):
    B, T, D = x.shape
    n = B * T
    x2 = x.reshape(n, D)
    cap = CAP_FACTOR * n // N_EXPERTS
    atok, a2 = router(x2, w_router)
    idx, dst, tab, kmax = select(a2, cap)
    y = moe_ffn(x2, idx, dst, w_gate, w_up, w_down, layer)
    return moe_combine(y, atok, tab, kmax[:, 0], x2, ln_g, ln_b).reshape(B, T, D)


def _trunk(x, p):
    for i in range(DEPTH):
        j = i // 2
        if i % 2 == 0:
            x = attn_layer(x, p["attn_w_qkv"], p["attn_w_o"], j, p["attn_sink"][j], p["ln_mix_g"][i], p["ln_mix_b"][i])
        else:
            x = dn_layer(x, p["dn_w_in"], p["dn_w_out"], j, p["dn_conv_w"][j], p["dn_a_log"][j], p["dn_dt_bias"][j],
                         p["dn_norm_w"][j], p["ln_mix_g"][i], p["ln_mix_b"][i])
        x = moe_layer(x, p["moe_w_router"][i], p["moe_w_gate"], p["moe_w_up"], p["moe_w_down"], i,
                      p["ln_ffn_g"][i], p["ln_ffn_b"][i])
    return x


def kernel(x_prompt, x_sample, attn_w_qkv, attn_w_o, attn_sink, dn_w_in, dn_conv_w, dn_a_log, dn_dt_bias, dn_norm_w,
           dn_w_out, moe_w_router, moe_w_gate, moe_w_up, moe_w_down, ln_mix_g, ln_mix_b, ln_ffn_g, ln_ffn_b):
    c = lambda w: w.astype(MXU)
    p = dict(
        attn_w_qkv=c(attn_w_qkv), attn_w_o=c(attn_w_o), attn_sink=attn_sink,
        dn_w_in=c(dn_w_in), dn_conv_w=dn_conv_w, dn_a_log=dn_a_log, dn_dt_bias=dn_dt_bias,
        dn_norm_w=dn_norm_w, dn_w_out=c(dn_w_out), moe_w_router=moe_w_router, moe_w_gate=c(moe_w_gate),
        moe_w_up=c(moe_w_up), moe_w_down=c(moe_w_down), ln_mix_g=ln_mix_g, ln_mix_b=ln_mix_b, ln_ffn_g=ln_ffn_g,
        ln_ffn_b=ln_ffn_b)
    return _trunk(x_prompt, p), _trunk(x_sample, p)
```

```python
import functools
import math

import jax
import jax.numpy as jnp
from jax import lax
from jax.experimental import pallas as pl
from jax.experimental.pallas import tpu as pltpu

f32, i32 = jnp.float32, jnp.int32
MXU = jnp.bfloat16

D_MODEL = 2048
DEPTH = 4
HQ, HKV, HD = 16, 4, 128
GROUP = HQ // HKV
BLOCK = 128
ROT_DIM = HD // 4
ROPE_THETA = 500000.0
HK, HV, DK, DV = 16, 32, 128, 128
QK_DIM, V_DIM = HK * DK, HV * DV
CONV_DIM = 2 * QK_DIM + V_DIM
CONV_K = 5
CHUNK = 64
N_EXPERTS = 16
D_EXPERT = 1024
CAP_FACTOR = 2
DEEPNORM_ALPHA = (2.0 * DEPTH) ** 0.25
LN_EPS = 1e-5
RMS_EPS = 1e-6

LANES = 128
V7X_VMEM_LIMIT = 56 * 1024 * 1024


def _params(*sem):
    return pltpu.CompilerParams(dimension_semantics=sem, vmem_limit_bytes=V7X_VMEM_LIMIT)


def _mm_kernel(a_ref, w_ref, o_ref):
    o_ref[...] = jnp.dot(a_ref[...].astype(MXU), w_ref[...], preferred_element_type=f32).astype(o_ref.dtype)


def matmul(a, w, layer, col0=0, ncols=None, out_dtype=f32, tm=1024, tn=1024):
    M, K = a.shape
    N = w.shape[2] - col0 if ncols is None else ncols
    tm, tn = min(tm, M), min(tn, N)
    assert M % tm == 0 and N % tn == 0 and col0 % tn == 0
    c0 = col0 // tn
    return pl.pallas_call(
        _mm_kernel,
        grid=(N // tn, M // tm),
        in_specs=[pl.BlockSpec((tm, K), lambda j, i: (i, 0)), pl.BlockSpec((None, K, tn), lambda j, i: (layer, 0, c0 + j))],
        out_specs=pl.BlockSpec((tm, tn), lambda j, i: (i, j)),
        out_shape=jax.ShapeDtypeStruct((M, N), out_dtype),
        compiler_params=_params("parallel", "parallel"),
        name="matmul",
    )(a, w)


def _layer_norm_rows(y, g, b):
    mu = jnp.mean(y, axis=-1, keepdims=True)
    yc = y - mu
    var = jnp.mean(yc * yc, axis=-1, keepdims=True)
    return yc * lax.rsqrt(var + LN_EPS) * g + b


def _mm_ln_kernel(a_ref, w_ref, x_ref, g_ref, b_ref, o_ref):
    h = jnp.dot(a_ref[...].astype(MXU), w_ref[...], preferred_element_type=f32)
    o_ref[...] = _layer_norm_rows(DEEPNORM_ALPHA * x_ref[...] + h, g_ref[...], b_ref[...])


def matmul_res_ln(a, w, layer, x, g, b, tm=256):
    M, K = a.shape
    D = w.shape[2]
    tm = min(tm, M)
    assert M % tm == 0
    return pl.pallas_call(
        _mm_ln_kernel,
        grid=(M // tm,),
        in_specs=[
            pl.BlockSpec((tm, K), lambda i: (i, 0)),
            pl.BlockSpec((None, K, D), lambda i: (layer, 0, 0), pipeline_mode=pl.Buffered(1)),
            pl.BlockSpec((tm, D), lambda i: (i, 0)),
            pl.BlockSpec((1, D), lambda i: (0, 0)),
            pl.BlockSpec((1, D), lambda i: (0, 0)),
        ],
        out_specs=pl.BlockSpec((tm, D), lambda i: (i, 0)),
        out_shape=jax.ShapeDtypeStruct((M, D), f32),
        compiler_params=_params("parallel"),
        name="matmul_res_ln",
    )(a, w, x, g.reshape(1, D), b.reshape(1, D))


def rope_table(T):
    half = ROT_DIM // 2
    inv = ROPE_THETA ** (-jnp.arange(0, ROT_DIM, 2, dtype=f32) / ROT_DIM)
    ang = jnp.arange(T, dtype=f32)[:, None] * inv[None, :]
    c, s = jnp.cos(ang), jnp.sin(ang)
    z = jnp.zeros((T, HD - ROT_DIM), f32)
    zh = jnp.zeros((T, half), f32)
    return jnp.concatenate([c, c, z + 1.0, -s, zh, z, zh, s, z], axis=1)


def _rope(t, tab):
    return t * tab[:, :HD] + pltpu.roll(t, HD - ROT_DIM // 2, 1) * tab[:, HD:2 * HD] + pltpu.roll(t, ROT_DIM // 2, 1) * tab[:, 2 * HD:]


def _attn_kernel(sink_ref, q_ref, kp_ref, kc_ref, kn_ref, vp_ref, vc_ref, vn_ref, tp_ref, tc_ref, tn_ref, o_ref):
    n = pl.program_id(1)
    nb = pl.num_programs(1)
    ii = lax.broadcasted_iota(i32, (GROUP * BLOCK, 3 * BLOCK), 0) % BLOCK
    jj = lax.broadcasted_iota(i32, (GROUP * BLOCK, 3 * BLOCK), 1)
    lo = jnp.where(n == 0, BLOCK, 0)
    hi = jnp.where(n == nb - 1, 2 * BLOCK, 3 * BLOCK)
    valid = (jj >= ii) & (jj <= ii + 2 * BLOCK) & (jj >= lo) & (jj < hi)
    tq = tc_ref[...]
    tabs = (tp_ref[...], tq, tn_ref[...])
    heads = range(HKV)
    hsl = [slice(h * HD, (h + 1) * HD) for h in heads]
    kbs = [jnp.concatenate([_rope(r[0, :, hs], t) for r, t in zip((kp_ref, kc_ref, kn_ref), tabs)], axis=0).astype(MXU)
           for hs in hsl]
    vbs = [jnp.concatenate([r[0, :, hs] for r in (vp_ref, vc_ref, vn_ref)], axis=0).astype(MXU) for hs in hsl]
    qhs = [jnp.concatenate([_rope(q_ref[0, :, (h * GROUP + g) * HD:(h * GROUP + g + 1) * HD], tq) for g in range(GROUP)],
                           axis=0).astype(MXU) for h in heads]
    ss = [jnp.where(valid, lax.dot_general(qh, kb, (((1,), (1,)), ((), ())), preferred_element_type=f32) * (HD ** -0.5),
                    -jnp.inf) for qh, kb in zip(qhs, kbs)]
    sks = [jnp.concatenate([jnp.full((BLOCK, 1), sink_ref[0, h * GROUP + g], f32) for g in range(GROUP)], axis=0)
           for h in heads]
    def lane_tiles(a):
        return [a[:, t * BLOCK:(t + 1) * BLOCK] for t in range(3)]

    ms = [jnp.maximum(jnp.max(functools.reduce(jnp.maximum, lane_tiles(s)), axis=-1, keepdims=True), sk)
          for s, sk in zip(ss, sks)]
    ps = [jnp.exp(s - m) for s, m in zip(ss, ms)]
    denoms = [jnp.sum(functools.reduce(jnp.add, lane_tiles(p)), axis=-1, keepdims=True) + jnp.exp(sk - m)
              for p, sk, m in zip(ps, sks, ms)]
    os_ = [jnp.dot((p / d).astype(MXU), vb, preferred_element_type=f32) for p, d, vb in zip(ps, denoms, vbs)]
    for h, o in zip(heads, os_):
        for g in range(GROUP):
            o_ref[0, :, (h * GROUP + g) * HD:(h * GROUP + g + 1) * HD] = o[g * BLOCK:(g + 1) * BLOCK].astype(o_ref.dtype)


def attention(qkv, sink):
    B, T, _ = qkv.shape
    out_dtype = MXU
    nb = T // BLOCK
    tab = rope_table(T)
    kcol, vcol = HQ // HKV, HQ // HKV + 1
    kvw = HKV * HD
    prev = lambda n: jnp.maximum(n - 1, 0)
    nxt = lambda n: jnp.minimum(n + 1, nb - 1)
    return pl.pallas_call(
        _attn_kernel,
        grid=(B, nb),
        in_specs=[
            pl.BlockSpec(memory_space=pltpu.SMEM),
            pl.BlockSpec((1, BLOCK, HQ * HD), lambda b, n: (b, n, 0)),
            pl.BlockSpec((1, BLOCK, kvw), lambda b, n: (b, prev(n), kcol)),
            pl.BlockSpec((1, BLOCK, kvw), lambda b, n: (b, n, kcol)),
            pl.BlockSpec((1, BLOCK, kvw), lambda b, n: (b, nxt(n), kcol)),
            pl.BlockSpec((1, BLOCK, kvw), lambda b, n: (b, prev(n), vcol)),
            pl.BlockSpec((1, BLOCK, kvw), lambda b, n: (b, n, vcol)),
            pl.BlockSpec((1, BLOCK, kvw), lambda b, n: (b, nxt(n), vcol)),
            pl.BlockSpec((BLOCK, 3 * HD), lambda b, n: (prev(n), 0)),
            pl.BlockSpec((BLOCK, 3 * HD), lambda b, n: (n, 0)),
            pl.BlockSpec((BLOCK, 3 * HD), lambda b, n: (nxt(n), 0)),
        ],
        out_specs=pl.BlockSpec((1, BLOCK, HQ * HD), lambda b, n: (b, n, 0)),
        out_shape=jax.ShapeDtypeStruct((B, T, HQ * HD), out_dtype),
        compiler_params=_params("parallel", "parallel"),
        name="attention",
    )(sink.reshape(1, HQ).astype(f32), qkv, qkv, qkv, qkv, qkv, qkv, qkv, tab, tab, tab)


def attn_layer(x, w_qkv, w_o, layer, sink, ln_g, ln_b):
    B, T, D = x.shape
    x2 = x.reshape(B * T, D)
    qkv = matmul(x2, w_qkv, layer).reshape(B, T, -1)
    o = attention(qkv, sink).reshape(B * T, HQ * HD)
    return matmul_res_ln(o, w_o, layer, x2, ln_g, ln_b).reshape(B, T, D)


def _silu(x):
    return x * jax.nn.sigmoid(x)


def _conv_kernel(xp_ref, xc_ref, xn_ref, w_ref, o_ref, *, l2, scale):
    i = pl.program_id(1)
    tt = xc_ref.shape[1]
    pad = CONV_K // 2
    xp = jnp.where(i == 0, 0.0, xp_ref[0])
    xn = jnp.where(i == pl.num_programs(1) - 1, 0.0, xn_ref[0])
    xe = jnp.concatenate([xp, xc_ref[0], xn], axis=0)
    acc = xe[8 - pad:8 - pad + tt] * w_ref[0:1, :]
    for j in range(1, CONV_K):
        acc = acc + xe[8 - pad + j:8 - pad + j + tt] * w_ref[j:j + 1, :]
    y = _silu(acc)
    for h in range(y.shape[1] // DK):
        seg = y[:, h * DK:(h + 1) * DK]
        if l2:
            seg = seg * lax.rsqrt(jnp.sum(seg * seg, axis=-1, keepdims=True) + RMS_EPS) * scale
        o_ref[0, :, h * DK:(h + 1) * DK] = seg


def _conv_part(proj, conv_w, col0, ncols, l2, scale, tt=512, cb=512):
    B, T, _ = proj.shape
    tt = min(tt, T)
    assert T % tt == 0 and col0 % cb == 0 and ncols % cb == 0
    c0 = col0 // cb
    r8 = tt // 8
    return pl.pallas_call(
        functools.partial(_conv_kernel, l2=l2, scale=scale),
        grid=(B, T // tt, ncols // cb),
        in_specs=[
            pl.BlockSpec((1, 8, cb), lambda b, i, c: (b, jnp.maximum(i * r8 - 1, 0), c0 + c)),
            pl.BlockSpec((1, tt, cb), lambda b, i, c: (b, i, c0 + c)),
            pl.BlockSpec((1, 8, cb), lambda b, i, c: (b, jnp.minimum((i + 1) * r8, T // 8 - 1), c0 + c)),
            pl.BlockSpec((CONV_K, cb), lambda b, i, c: (0, c0 + c)),
        ],
        out_specs=pl.BlockSpec((1, tt, cb), lambda b, i, c: (b, i, c)),
        out_shape=jax.ShapeDtypeStruct((B, T, ncols), f32),
        compiler_params=_params("parallel", "parallel", "parallel"),
        name="dn_conv",
    )(proj, proj, proj, conv_w)


def conv_qkv(proj, conv_w):
    q = _conv_part(proj, conv_w, 0, QK_DIM, True, DK ** -0.5)
    k = _conv_part(proj, conv_w, QK_DIM, QK_DIM, True, 1.0)
    v = _conv_part(proj, conv_w, 2 * QK_DIM, V_DIM, False, 1.0)
    return q, k, v


def _gates_kernel(ab_ref, alog_ref, dtb_ref, o_ref):
    x = ab_ref[0]
    tt = x.shape[0]
    g = -jnp.exp(alog_ref[...]) * jax.nn.softplus(x + dtb_ref[...])
    beta = jax.nn.sigmoid(x)
    ri = lax.broadcasted_iota(i32, (tt, tt), 0)
    ci = lax.broadcasted_iota(i32, (tt, tt), 1)
    same = (ri // CHUNK) == (ci // CHUNK)
    lp = jnp.where(same & (ci <= ri), 1.0, 0.0).astype(f32)
    ls = jnp.where(same & (ci >= ri), 1.0, 0.0).astype(f32)
    gp = jnp.dot(lp, g, precision=lax.Precision.HIGHEST, preferred_element_type=f32)
    gs = jnp.dot(ls, g, precision=lax.Precision.HIGHEST, preferred_element_type=f32)
    lane = lax.broadcasted_iota(i32, x.shape, 1)
    cum = jnp.where(lane < 2 * HV, gp, gs)
    o_ref[0] = jnp.where((lane % (2 * HV)) < HV, cum, beta)


def gates(ab, a_log, dt_bias, tt=256):
    B, T, W = ab.shape
    tt = min(tt, T)
    z = jnp.zeros((2, HV), f32)
    alog = jnp.concatenate([a_log.astype(f32), z], axis=1).reshape(1, W)
    dtb = jnp.concatenate([dt_bias.astype(f32), z], axis=1).reshape(1, W)
    return pl.pallas_call(
        _gates_kernel,
        grid=(B, T // tt),
        in_specs=[pl.BlockSpec((1, tt, W), lambda b, i: (b, i, 0)), pl.BlockSpec((1, W), lambda b, i: (0, 0)),
                  pl.BlockSpec((1, W), lambda b, i: (0, 0))],
        out_specs=pl.BlockSpec((1, tt, W), lambda b, i: (b, i, 0)),
        out_shape=jax.ShapeDtypeStruct((B, T, W), f32),
        compiler_params=_params("parallel", "parallel"),
        name="dn_gates",
    )(ab, alog, dtb)


PACK = 4


def _mdot(a, b):
    return jnp.dot(a.astype(MXU), b.astype(MXU), preferred_element_type=f32)


def _packed_unit_tri_inverse(Ls, eye, bd_mask):
    c = Ls[0].shape[0]

    def bd(m):
        return jnp.where(bd_mask, jnp.concatenate([m] * PACK, axis=0), 0.0)

    ms = [-L for L in Ls]
    ps = [jnp.where(eye, 1.0, 0.0) + m for m in ms]
    ms = [_mdot(m, bd(m)) for m in ms]
    for _ in range(int(math.log2(c)) - 2):
        rs = [_mdot(jnp.concatenate([m, p], axis=0), bd(m)) for m, p in zip(ms, ps)]
        ms, ps = [r[:c] for r in rs], [p + r[c:] for p, r in zip(ps, rs)]
    return [p + _mdot(p, bd(m)) for m, p in zip(ms, ps)]


def _delta_kernel_packed(qf, kf, vf, gf, qb, kb, vb, gb, of, ob, s_ref, *, hb):
    C = CHUNK
    rep = HV // HK
    W = PACK * C
    h0 = pl.program_id(1) * hb

    @pl.when(pl.program_id(2) == 0)
    def _():
        s_ref[...] = jnp.zeros_like(s_ref)

    tb = qf.shape[1]
    nc = tb // C
    row = lax.broadcasted_iota(i32, (C, W), 0)
    col = lax.broadcasted_iota(i32, (C, W), 1) % C
    blk = lax.broadcasted_iota(i32, (C, W), 1) // C
    eye = row == col
    bd_mask = (lax.broadcasted_iota(i32, (W, W), 0) // C) == (lax.broadcasted_iota(i32, (W, W), 1) // C)
    blk2 = lax.broadcasted_iota(i32, (DK + C, W), 1) // C
    def side_by_side(parts):
        out = jnp.broadcast_to(parts[-1], (C, W))
        for b in range(PACK - 2, -1, -1):
            out = jnp.where(blk == b, parts[b], out)
        return out

    groups = []
    for reverse, (q_ref, k_ref, v_ref, g_ref) in enumerate(((qf, kf, vf, gf), (qb, kb, vb, gb))):
        incl = (row <= col) if reverse else (row >= col)
        strict = (row < col) if reverse else (row > col)
        gt = g_ref[0]
        lane = lax.broadcasted_iota(i32, gt.shape, 1)
        for h in range(hb):
            lane_g = [(2 * HV if reverse else 0) + rep * (h0 + h) + j for j in range(rep)]
            gcols = [jnp.sum(jnp.where(lane == lg, gt, 0.0), axis=1, keepdims=True) for lg in lane_g]
            bcols = [jnp.sum(jnp.where(lane == lg + HV, gt, 0.0), axis=1, keepdims=True) for lg in lane_g]
            for c0 in range(0, nc, PACK // rep):
                members = [(c0 + ci, j) for ci in range(PACK // rep) for j in range(rep)]
                kcs = {c: k_ref[0, c * C:(c + 1) * C, h * DK:(h + 1) * DK] for c, _ in members}
                qcs = {c: q_ref[0, c * C:(c + 1) * C, h * DK:(h + 1) * DK] for c, _ in members}
                gc = [gcols[j][c * C:(c + 1) * C] for c, j in members]
                bc = [bcols[j][c * C:(c + 1) * C] for c, j in members]
                g4, b4 = side_by_side(gc), side_by_side(bc)
                gram = jnp.concatenate(
                    [lax.dot_general(jnp.concatenate([kcs[c], qcs[c]], axis=0).astype(MXU),
                                     jnp.concatenate([kcs[c]] * rep, axis=0).astype(MXU), (((1,), (1,)), ((), ())),
                                     preferred_element_type=f32) for c in sorted(kcs)], axis=1)
                kk, qk = gram[:C], gram[C:]
                kT = jnp.concatenate([kcs[c] for c, _ in members], axis=0).T
                grow = jnp.sum(jnp.where(eye, g4, 0.0), axis=0, keepdims=True)
                glast = g4[0:1] if reverse else g4[C - 1:C]
                decay = jnp.exp(jnp.where(incl, g4 - grow, -jnp.inf))
                rhs = jnp.concatenate(
                    [jnp.concatenate([v_ref[0, c * C:(c + 1) * C, (h * rep + j) * DV:(h * rep + j + 1) * DV] * bc[b],
                                      kcs[c] * (bc[b] * jnp.exp(gc[b]))], axis=1) for b, (c, j) in enumerate(members)],
                    axis=0)
                groups.append(dict(
                    reverse=reverse, h=h, members=members, L=jnp.where(strict, b4 * kk * decay, 0.0), rhs=rhs,
                    lhs2=jnp.concatenate([kT * jnp.exp(glast - grow), qk * decay], axis=0),
                    qeg=[qcs[c] * jnp.exp(gc[b]) for b, (c, j) in enumerate(members)],
                    egl=[jnp.exp(gc[b][0:1] if reverse else gc[b][C - 1:C]) for b in range(PACK)]))
    tinvs = _packed_unit_tri_inverse([g["L"] for g in groups], eye, bd_mask)
    sols = [_mdot(jnp.where(bd_mask, jnp.concatenate([t] * PACK, axis=0), 0.0), g["rhs"]).astype(MXU)
            for t, g in zip(tinvs, groups)]
    local = {}
    for g, sol in zip(groups, sols):
        for b, (c, j) in enumerate(g["members"]):
            r2 = _mdot(jnp.where(blk2 == b, g["lhs2"], 0.0), sol)
            aq = jnp.concatenate([r2[:DK, DV:], g["qeg"][b] - r2[DK:, DV:]], axis=0).astype(MXU)
            local[(g["reverse"], g["h"], j, c)] = (aq, r2[:DK, :DV], r2[DK:, :DV], g["egl"][b])
    nch = 2 * hb * rep
    S = [s_ref[ch] for ch in range(nch)]
    for step in range(nc):
        for ch in range(nch):
            reverse, hcol = divmod(ch, hb * rep)
            h, j = divmod(hcol, rep)
            c = nc - 1 - step if reverse else step
            aq, bm, op, egl = local[(reverse, h, j, c)]
            r = jnp.dot(aq, S[ch].astype(MXU), preferred_element_type=f32)
            (ob if reverse else of)[0, c * C:(c + 1) * C, hcol * DV:(hcol + 1) * DV] = r[DK:] + op
            S[ch] = S[ch] * egl - r[:DK] + bm
    for ch in range(nch):
        s_ref[ch] = S[ch]


def delta_rule(q, k, v, gt, tb=512, hb=4):
    B, T, _ = q.shape
    tb = min(tb, T)
    nT = T // tb
    rep = HV // HK
    fwd = lambda b, h, i: (b, i, h)
    bwd = lambda b, h, i: (b, nT - 1 - i, h)
    fwd0 = lambda b, h, i: (b, i, 0)
    bwd0 = lambda b, h, i: (b, nT - 1 - i, 0)
    qk_spec = lambda m: pl.BlockSpec((1, tb, hb * DK), m)
    v_spec = lambda m: pl.BlockSpec((1, tb, hb * rep * DV), m)
    g_spec = lambda m: pl.BlockSpec((1, tb, gt.shape[2]), m)
    o_shape = jax.ShapeDtypeStruct((B, T, V_DIM), f32)
    return pl.pallas_call(
        functools.partial(_delta_kernel_packed, hb=hb),
        grid=(B, HK // hb, nT),
        in_specs=[qk_spec(fwd), qk_spec(fwd), v_spec(fwd), g_spec(fwd0), qk_spec(bwd), qk_spec(bwd), v_spec(bwd), g_spec(bwd0)],
        out_specs=[v_spec(fwd), v_spec(bwd)],
        out_shape=[o_shape, o_shape],
        scratch_shapes=[pltpu.VMEM((2 * hb * rep, DK, DV), f32)],
        compiler_params=_params("parallel", "parallel", "arbitrary"),
        name="dn_delta",
    )(q, k, v, gt, q, k, v, gt)


def _gnorm_mm_ln_kernel(of_ref, ob_ref, z_ref, nw_ref, w_ref, x_ref, g_ref, b_ref, o_ref, a_ref):
    for h in range(of_ref.shape[1] // DV):
        hs = slice(h * DV, (h + 1) * DV)
        seg = of_ref[:, hs] + ob_ref[:, hs]
        y = seg * lax.rsqrt(jnp.mean(seg * seg, axis=-1, keepdims=True) + RMS_EPS) * nw_ref[...]
        a_ref[:, hs] = (y * _silu(z_ref[:, hs])).astype(a_ref.dtype)
    h = jnp.dot(a_ref[...], w_ref[...], preferred_element_type=f32)
    o_ref[...] = _layer_norm_rows(DEEPNORM_ALPHA * x_ref[...] + h, g_ref[...], b_ref[...])


def gated_norm_matmul_res_ln(of, ob, z, norm_w, w, layer, x, g, b, tm=128):
    M, K = of.shape
    D = w.shape[2]
    tm = min(tm, M)
    assert M % tm == 0
    aspec = pl.BlockSpec((tm, K), lambda i: (i, 0))
    vec = pl.BlockSpec((1, D), lambda i: (0, 0))
    return pl.pallas_call(
        _gnorm_mm_ln_kernel,
        grid=(M // tm,),
        in_specs=[aspec, aspec, aspec, pl.BlockSpec((1, DV), lambda i: (0, 0)),
                  pl.BlockSpec((None, K, D), lambda i: (layer, 0, 0), pipeline_mode=pl.Buffered(1)),
                  pl.BlockSpec((tm, D), lambda i: (i, 0)), vec, vec],
        out_specs=pl.BlockSpec((tm, D), lambda i: (i, 0)),
        out_shape=jax.ShapeDtypeStruct((M, D), f32),
        scratch_shapes=[pltpu.VMEM((tm, K), MXU)],
        compiler_params=_params("parallel"),
        name="gnorm_matmul_res_ln",
    )(of, ob, z, norm_w.reshape(1, DV).astype(f32), w, x, g.reshape(1, D), b.reshape(1, D))


def dn_layer(x, w_in, w_out, layer, conv_w, a_log, dt_bias, norm_w, ln_g, ln_b):
    B, T, D = x.shape
    x2 = x.reshape(B * T, D)
    pc = matmul(x2, w_in, layer, 0, CONV_DIM).reshape(B, T, CONV_DIM)
    z = matmul(x2, w_in, layer, CONV_DIM, V_DIM)
    ab = matmul(x2, w_in, layer, CONV_DIM + V_DIM, 4 * HV).reshape(B, T, -1)
    q, k, v = conv_qkv(pc, conv_w)
    gt = gates(ab, a_log, dt_bias)
    of, ob = delta_rule(q, k, v, gt)
    return gated_norm_matmul_res_ln(of.reshape(B * T, V_DIM), ob.reshape(B * T, V_DIM), z, norm_w, w_out, layer, x2,
                                    ln_g, ln_b).reshape(B, T, D)


EXACT = jnp.bfloat16


def _split3(x):
    hi = x.astype(MXU)
    return hi, (x - hi.astype(f32)).astype(MXU)


def _router_kernel(x_ref, w_ref, atok_ref, a2_ref):
    xh, xl = _split3(x_ref[...])
    wh, wl = _split3(w_ref[...])
    dot = functools.partial(jnp.dot, preferred_element_type=f32)
    logits = dot(xh, wh) + (dot(xl, wh) + dot(xh, wl))
    lane = lax.broadcasted_iota(i32, logits.shape, 1)
    lm = jnp.where(lane < N_EXPERTS, logits, -jnp.inf)
    e = jnp.exp(lm - jnp.max(lm, axis=-1, keepdims=True))
    aff = e / jnp.sum(e, axis=-1, keepdims=True)
    atok_ref[...] = aff
    for j in range(aff.shape[0] // LANES):
        bt = aff[j * LANES:(j + 1) * LANES, :].T
        for ex in range(N_EXPERTS):
            a2_ref[j:j + 1, ex * LANES:(ex + 1) * LANES] = bt[ex:ex + 1, :]


def router(x2, w_router, tm=1024):
    n, D = x2.shape
    tm = min(tm, n)
    w = jnp.zeros((D, LANES), f32).at[:, :N_EXPERTS].set(w_router.astype(f32))
    return pl.pallas_call(
        _router_kernel,
        grid=(n // tm,),
        in_specs=[pl.BlockSpec((tm, D), lambda i: (i, 0)), pl.BlockSpec((D, LANES), lambda i: (0, 0))],
        out_specs=[pl.BlockSpec((tm, LANES), lambda i: (i, 0)), pl.BlockSpec((tm // LANES, N_EXPERTS * LANES), lambda i: (i, 0))],
        out_shape=[jax.ShapeDtypeStruct((n, LANES), f32), jax.ShapeDtypeStruct((n // LANES, N_EXPERTS * LANES), f32)],
        compiler_params=_params("parallel"),
        name="moe_router",
    )(x2, w)


def _per_expert(row, op):
    parts = []
    for ex in range(N_EXPERTS):
        s = op(row[:, ex * LANES:(ex + 1) * LANES], axis=1, keepdims=True)
        parts.append(jnp.broadcast_to(s, (1, LANES)))
    return jnp.concatenate(parts, axis=1)


SELECT_UNROLL = 8


def _select_kernel(a_ref, idx_ref, dst_ref, tab_ref, kmax_ref, *, cap, n):
    R = a_ref.shape[0]
    W = N_EXPERTS * LANES
    bits = pltpu.bitcast(a_ref[...], i32)
    capf = jnp.float32(cap)

    def count(mask):
        return _per_expert(jnp.sum(jnp.where(mask, 1.0, 0.0), axis=0, keepdims=True), jnp.sum)

    def bit_step(t, ans):
        cand = ans | lax.shift_left(jnp.int32(1), 29 - t)
        return jnp.where(count(bits >= cand) >= capf, cand, ans)

    thr = lax.fori_loop(0, 30, bit_step, jnp.zeros((1, W), i32))
    gt = bits > thr
    eq = bits == thr
    need = capf - count(gt)

    lane_r = lax.broadcasted_iota(i32, (LANES, LANES), 0)
    lane_c = lax.broadcasted_iota(i32, (LANES, LANES), 1)
    upper = jnp.where(lane_r <= lane_c, 1.0, 0.0).astype(EXACT)
    row_r = lax.broadcasted_iota(i32, (R, R), 0)
    row_c = lax.broadcasted_iota(i32, (R, R), 1)
    below = jnp.where(row_c < row_r, 1.0, 0.0).astype(EXACT)

    def prefix(slab):
        lc = jnp.dot(slab.astype(EXACT), upper, preferred_element_type=f32)
        tot = jnp.broadcast_to(lc[:, LANES - 1:LANES], (R, LANES))
        off = jnp.dot(below, tot.astype(EXACT), preferred_element_type=f32)
        return lc, off, tot

    tok = (lax.broadcasted_iota(i32, (R, LANES), 0) * LANES + lax.broadcasted_iota(i32, (R, LANES), 1))
    sub_r = lax.broadcasted_iota(i32, (R, LANES), 0).astype(f32)
    sub_l = lax.broadcasted_iota(i32, (LANES, LANES), 0).astype(f32)
    lane_p = lax.broadcasted_iota(i32, (1, LANES), 1)
    nsel = jnp.zeros((R, LANES), f32)
    cut_parts = []
    for ex in range(N_EXPERTS):
        es = slice(ex * LANES, (ex + 1) * LANES)
        eq_e = jnp.where(eq[:, es], 1.0, 0.0)
        lc, off, _ = prefix(eq_e)
        take = eq[:, es] & (lc + off <= need[:, es])
        cut_parts.append(jnp.broadcast_to(jnp.max(jnp.max(jnp.where(take, tok, -1), axis=0, keepdims=True), axis=1, keepdims=True), (1, LANES)))
        sel_e = jnp.where(gt[:, es] | take, 1.0, 0.0)
        lc, off, tot = prefix(sel_e)
        rank_e = nsel
        nsel = nsel + sel_e
        hi = jnp.floor(off * (1.0 / LANES))
        lo = off - hi * LANES
        table = jnp.concatenate([lc.T, rank_e.T, hi.T[0:8], lo.T[0:8]], axis=0).astype(EXACT)
        end = off + tot

        def chunk(c, carry):
            c = lax.convert_element_type(c, i32)
            p =(c * LANES + lane_p).astype(f32)
            jp = jnp.sum(jnp.where(end <= p, 1.0, 0.0), axis=0, keepdims=True)
            onehot = jnp.where(sub_r == jp, 1.0, 0.0).astype(EXACT)
            g = jnp.dot(table, onehot, preferred_element_type=f32)
            p_loc = p - (g[2 * LANES:2 * LANES + 1] * LANES + g[2 * LANES + 8:2 * LANES + 9])
            lp = jnp.sum(jnp.where(g[:LANES] <= p_loc, 1.0, 0.0), axis=0, keepdims=True)
            rk = jnp.sum(jnp.where(sub_l == lp, g[LANES:2 * LANES], 0.0), axis=0, keepdims=True)
            t = jp * LANES + lp
            col = pl.ds(pl.multiple_of(c * LANES, LANES), LANES)
            idx_ref[ex:ex + 1, col] = t.astype(i32)
            dst_ref[ex:ex + 1, col] = (rk * n + t).astype(i32)
            return carry

        lax.fori_loop(0, cap // LANES, chunk, 0, unroll=SELECT_UNROLL)

    diag = lax.broadcasted_iota(i32, (N_EXPERTS, LANES), 0) == lax.broadcasted_iota(i32, (N_EXPERTS, LANES), 1)
    thr_rows = jnp.concatenate([thr[:, ex * LANES:(ex + 1) * LANES] for ex in range(N_EXPERTS)], axis=0)
    cut_rows = jnp.concatenate(cut_parts, axis=0)
    tab_ref[...] = jnp.zeros_like(tab_ref)
    tab_ref[0:1, :] = jnp.sum(jnp.where(diag, thr_rows, 0), axis=0, keepdims=True)
    tab_ref[1:2, :] = jnp.sum(jnp.where(diag, cut_rows, 0), axis=0, keepdims=True)
    kmax_ref[...] = jnp.broadcast_to(jnp.max(nsel, axis=1, keepdims=True), (R, LANES)).astype(i32)


def select(a2, cap):
    R = a2.shape[0]
    n = R * LANES
    assert cap % LANES == 0
    return pl.pallas_call(
        functools.partial(_select_kernel, cap=cap, n=n),
        out_shape=[jax.ShapeDtypeStruct((N_EXPERTS, cap), i32), jax.ShapeDtypeStruct((N_EXPERTS, cap), i32),
                   jax.ShapeDtypeStruct((8, LANES), i32), jax.ShapeDtypeStruct((R, LANES), i32)],
        compiler_params=pltpu.CompilerParams(vmem_limit_bytes=V7X_VMEM_LIMIT),
        name="moe_select",
    )(a2)


ROW_DMA_UNROLL = 8

def _ffn_kernel(idx_ref, nidx_ref, dst_ref, x_hbm, wg_ref, wu_ref, wd_ref, y_hbm, xbuf, ybuf, gsem, ssem):
    tm = xbuf.shape[1]
    s = pl.program_id(0) * pl.num_programs(1) + pl.program_id(1)
    last = pl.num_programs(0) * pl.num_programs(1) - 1
    slot = s % 2

    def gather(ref, r, sl):
        return pltpu.make_async_copy(x_hbm.at[pl.ds(ref[0, 0, r], 1), :], xbuf.at[sl, pl.ds(r, 1), :], gsem.at[sl])

    def scatter(r, sl):
        return pltpu.make_async_copy(ybuf.at[sl, pl.ds(r, 1), :], y_hbm.at[pl.ds(dst_ref[0, 0, r], 1), :], ssem.at[sl])

    def each_row(fn):
        def body(r, c):
            fn(r)
            return c
        lax.fori_loop(0, tm, body, 0, unroll=ROW_DMA_UNROLL)

    @pl.when(s == 0)
    def _():
        each_row(lambda r: gather(idx_ref, r, 0).start())

    each_row(lambda r: gather(idx_ref, r, slot).wait())

    @pl.when(s >= 2)
    def _():
        each_row(lambda r: scatter(r, slot).wait())

    for r in range(tm):
        gather(nidx_ref, r, 1 - slot).start()
    xe = xbuf[slot].astype(MXU)
    gate = jnp.dot(xe, wg_ref[...], preferred_element_type=f32)
    up = jnp.dot(xe, wu_ref[...], preferred_element_type=f32)
    h = (_silu(gate) * up).astype(MXU)
    ybuf[slot] = jnp.dot(h, wd_ref[...], preferred_element_type=f32)
    for r in range(tm):
        scatter(r, slot).start()

    @pl.when(s == last)
    def _():
        each_row(lambda r: gather(nidx_ref, r, 1 - slot).wait())

        @pl.when(s >= 1)
        def _():
            each_row(lambda r: scatter(r, 1 - slot).wait())

        each_row(lambda r: scatter(r, slot).wait())


def moe_ffn(x2, idx, dst, w_gate, w_up, w_down, layer, tm=256):
    n, D = x2.shape
    E, cap = idx.shape
    F = w_gate.shape[3]
    tm = min(tm, cap)
    nc = cap // tm
    ispec = pl.BlockSpec((1, 1, tm), lambda e, c: (e * nc + c, 0, 0), memory_space=pltpu.SMEM)
    nspec = pl.BlockSpec((1, 1, tm), lambda e, c: (jnp.minimum(e * nc + c + 1, E * nc - 1), 0, 0), memory_space=pltpu.SMEM)
    idx3 = idx.reshape(E * nc, 1, tm)
    return pl.pallas_call(
        _ffn_kernel,
        grid=(E, nc),
        in_specs=[
            ispec, nspec, ispec,
            pl.BlockSpec(memory_space=pl.ANY),
            pl.BlockSpec((None, None, D, F), lambda e, c: (layer, e, 0, 0)),
            pl.BlockSpec((None, None, D, F), lambda e, c: (layer, e, 0, 0)),
            pl.BlockSpec((None, None, F, D), lambda e, c: (layer, e, 0, 0)),
        ],
        out_specs=pl.BlockSpec(memory_space=pl.ANY),
        out_shape=jax.ShapeDtypeStruct((E * n, D), f32),
        scratch_shapes=[pltpu.VMEM((2, tm, D), f32), pltpu.VMEM((2, tm, D), f32), pltpu.SemaphoreType.DMA((2,)),
                        pltpu.SemaphoreType.DMA((2,))],
        compiler_params=_params("arbitrary", "arbitrary"),
        name="moe_ffn",
    )(idx3, idx3, dst.reshape(E * nc, 1, tm), x2, w_gate, w_up, w_down)


COMBINE_COLS = 256


def _combine_kernel(kmax_ref, y_hbm, atok_ref, tab_ref, x_ref, g_ref, b_ref, o_ref, ybuf, gbuf, hbuf, sem):
    i = pl.program_id(0)
    nt = pl.num_programs(0)
    tt, D = x_ref.shape
    half = i % 2
    km = kmax_ref[i]

    def fetch(tile, k, hf):
        return pltpu.make_async_copy(y_hbm.at[pl.ds(k * (nt * tt) + tile * tt, tt), :], ybuf.at[hf, k], sem.at[hf, k])

    def each_slot(count, fn):
        def body(k, c):
            fn(k)
            return c
        lax.fori_loop(0, count, body, 0)

    @pl.when(i == 0)
    def _():
        each_slot(km, lambda k: fetch(0, k, 0).start())

    @pl.when(i + 1 < nt)
    def _():
        each_slot(kmax_ref[jnp.minimum(i + 1, nt - 1)], lambda k: fetch(i + 1, k, 1 - half).start())

    aff = atok_ref[...]
    bits = pltpu.bitcast(aff, i32)
    lane = lax.broadcasted_iota(i32, aff.shape, 1)
    tok = i * tt + lax.broadcasted_iota(i32, aff.shape, 0)
    thr, cut = tab_ref[0:1, :], tab_ref[1:2, :]
    sel = (lane < N_EXPERTS) & ((bits > thr) | ((bits == thr) & (tok <= cut)))
    before = jnp.where(lax.broadcasted_iota(i32, (LANES, LANES), 0) < lax.broadcasted_iota(i32, (LANES, LANES), 1), 1.0, 0.0)
    rank = jnp.dot(jnp.where(sel, 1.0, 0.0).astype(EXACT), before.astype(EXACT), preferred_element_type=f32)

    def slot_gate(k):
        mine = sel & (rank == lax.convert_element_type(k, f32))
        gk = jnp.sum(jnp.where(mine, aff, 0.0), axis=1, keepdims=True)
        has = jnp.sum(jnp.where(mine, 1.0, 0.0), axis=1, keepdims=True) > 0.0
        gbuf[k] = jnp.broadcast_to(jnp.where(has, gk, -1.0), (tt, LANES))

    each_slot(km, slot_gate)
    each_slot(km, lambda k: fetch(i, k, half).wait())
    reps = COMBINE_COLS // LANES
    for cc in range(D // COMBINE_COLS):
        cs = slice(cc * COMBINE_COLS, (cc + 1) * COMBINE_COLS)

        def add_slot(k, acc):
            gk = jnp.concatenate([gbuf[k]] * reps, axis=1)
            return acc + jnp.where(gk >= 0.0, ybuf[half, k, :, cs], 0.0) * gk

        hbuf[:, cs] = lax.fori_loop(0, km, add_slot, jnp.zeros((tt, COMBINE_COLS), f32))
    o_ref[...] = _layer_norm_rows(DEEPNORM_ALPHA * x_ref[...] + hbuf[...], g_ref[...], b_ref[...])


def moe_combine(y, atok, tab, kmax, x2, g, b, tt=LANES):
    n, D = x2.shape
    assert tt == LANES
    return pl.pallas_call(
        _combine_kernel,
        grid_spec=pltpu.PrefetchScalarGridSpec(
            num_scalar_prefetch=1,
            grid=(n // tt,),
            in_specs=[
                pl.BlockSpec(memory_space=pl.ANY),
                pl.BlockSpec((tt, LANES), lambda i, km: (i, 0)),
                pl.BlockSpec((8, LANES), lambda i, km: (0, 0)),
                pl.BlockSpec((tt, D), lambda i, km: (i, 0)),
                pl.BlockSpec((1, D), lambda i, km: (0, 0)),
                pl.BlockSpec((1, D), lambda i, km: (0, 0)),
            ],
            out_specs=pl.BlockSpec((tt, D), lambda i, km: (i, 0)),
            scratch_shapes=[pltpu.VMEM((2, N_EXPERTS, tt, D), f32), pltpu.VMEM((N_EXPERTS, tt, LANES), f32),
                            pltpu.VMEM((tt, D), f32), pltpu.SemaphoreType.DMA((2, N_EXPERTS))],
        ),
        out_shape=jax.ShapeDtypeStruct((n, D), f32),
        compiler_params=_params("arbitrary"),
        name="moe_combine",
    )(kmax, y, atok, tab, x2, g.reshape(1, D), b.reshape(1, D))


def moe_layer(x, w_router, w_gate, w_up, w_down, layer, ln_g, ln_b):
    B, T, D = x.shape
    n = B * T
    x2 = x.reshape(n, D)
    cap = CAP_FACTOR * n // N_EXPERTS
    atok, a2 = router(x2, w_router)
    idx, dst, tab, kmax = select(a2, cap)
    y = moe_ffn(x2, idx, dst, w_gate, w_up, w_down, layer)
    return moe_combine(y, atok, tab, kmax[:, 0], x2, ln_g, ln_b).reshape(B, T, D)


def _trunk(x, p):
    for i in range(DEPTH):
        j = i // 2
        if i % 2 == 0:
            x = attn_layer(x, p["attn_w_qkv"], p["attn_w_o"], j, p["attn_sink"][j], p["ln_mix_g"][i], p["ln_mix_b"][i])
        else:
            x = dn_layer(x, p["dn_w_in"], p["dn_w_out"], j, p["dn_conv_w"][j], p["dn_a_log"][j], p["dn_dt_bias"][j],
                         p["dn_norm_w"][j], p["ln_mix_g"][i], p["ln_mix_b"][i])
        x = moe_layer(x, p["moe_w_router"][i], p["moe_w_gate"], p["moe_w_up"], p["moe_w_down"], i,
                      p["ln_ffn_g"][i], p["ln_ffn_b"][i])
    return x


def kernel(x_prompt, x_sample, attn_w_qkv, attn_w_o, attn_sink, dn_w_in, dn_conv_w, dn_a_log, dn_dt_bias, dn_norm_w,
           dn_w_out, moe_w_router, moe_w_gate, moe_w_up, moe_w_down, ln_mix_g, ln_mix_b, ln_ffn_g, ln_ffn_b):
    c = lambda w: w.astype(MXU)
    p = dict(
        attn_w_qkv=c(attn_w_qkv), attn_w_o=c(attn_w_o), attn_sink=attn_sink,
        dn_w_in=c(dn_w_in), dn_conv_w=dn_conv_w, dn_a_log=dn_a_log, dn_dt_bias=dn_dt_bias,
        dn_norm_w=dn_norm_w, dn_w_out=c(dn_w_out), moe_w_router=moe_w_router, moe_w_gate=c(moe_w_gate),
        moe_w_up=c(moe_w_up), moe_w_down=c(moe_w_down), ln_mix_g=ln_mix_g, ln_mix_b=ln_mix_b, ln_ffn_g=ln_ffn_g,
        ln_ffn_b=ln_ffn_b)
    return _trunk(x_prompt, p), _trunk(x_sample, p)
```
